```python
import math
import jax, jax.numpy as jnp
from jax import lax
import numpy as np

D_MODEL = 1024
BATCH = 2
SEQ = 16384
DEPTH = 2

N_EVEN = (DEPTH + 1) // 2
N_ODD = DEPTH // 2

M_WIDTH = D_MODEL // 2
M_HEADS = 4
M_HEAD_DIM = M_WIDTH // M_HEADS
M_CONV = 4
QKV_BLOCK = 4
M_CHUNK = 64
S_WIDTH = D_MODEL // 2
S_GROUP = 16
S_GROUPS = S_WIDTH // S_GROUP
S_STATE = 64
DT_MIN = 1e-3
DT_MAX = 1e-1
IN_EVEN = 2 * M_WIDTH + S_WIDTH
MIX_EVEN = M_WIDTH + S_WIDTH
H_EXPAND = 128
H_HEADS = D_MODEL // H_EXPAND
H_WIDTH = H_HEADS * H_EXPAND
H_CHUNK = 64
IN_ODD = 4 * H_WIDTH
D_FF = -(-8 * D_MODEL // (3 * 256)) * 256
EPS = 1e-6

kernel_name = 'hybrid_mlstm_s5_hgrn2_trunk'


def rmsnorm(x, g):
    xf = x.astype(jnp.float32)
    return xf * lax.rsqrt(jnp.mean(xf * xf, axis=-1, keepdims=True) + EPS) * g


def causal_dwconv(x, w, b):
    K, Cc = w.shape
    y = lax.conv_general_dilated(x, w[:, None, :].astype(x.dtype), window_strides=(1,), padding=[(K - 1, 0)],
                                 dimension_numbers=('NWC', 'WIO', 'NWC'), feature_group_count=Cc)
    return y + b


def swiglu(x, w1, w3, w2):
    return (jax.nn.silu(x @ w1) * (x @ w3)) @ w2


def mlstm_chunkwise(q, k, v, i_pre, f_pre):
    f32 = jnp.float32
    Bsz, H, L, Dh = q.shape
    C = M_CHUNK
    NC = L // C
    q, k, v = (t.astype(f32).reshape(Bsz, H, NC, C, Dh) for t in (q, k, v))
    log_f = jax.nn.log_sigmoid(f_pre.astype(f32)).reshape(Bsz, H, NC, C)
    log_i = i_pre.astype(f32).reshape(Bsz, H, NC, C)
    b = jnp.cumsum(log_f, axis=-1)
    b_last = b[..., -1]
    causal = jnp.tril(jnp.ones((C, C), dtype=bool))
    d_log = jnp.where(causal, b[..., :, None] - b[..., None, :] + log_i[..., None, :], -jnp.inf)
    m_intra = jnp.max(d_log, axis=-1)
    a = b_last[..., None] - b + log_i
    a_max = jnp.max(a, axis=-1)
    kw = k * jnp.exp(a - a_max[..., None])[..., None]
    U = jnp.einsum('bhcsk,bhcsv->bhckv', kw, v)
    un = jnp.sum(kw, axis=-2)

    def step(carry, inp):
        c_st, n_st, m_st = carry
        u_c, un_c, am_c, bl_c = inp
        m_new = jnp.maximum(bl_c + m_st, am_c)
        s_old = jnp.exp(bl_c + m_st - m_new)
        s_new = jnp.exp(am_c - m_new)
        c_new = s_old[..., None, None] * c_st + s_new[..., None, None] * u_c
        n_new = s_old[..., None] * n_st + s_new[..., None] * un_c
        return (c_new, n_new, m_new), (c_st, n_st, m_st)

    init = (jnp.zeros((Bsz, H, Dh, Dh), f32), jnp.zeros((Bsz, H, Dh), f32), jnp.zeros((Bsz, H), f32))
    xs = (jnp.moveaxis(U, 2, 0), jnp.moveaxis(un, 2, 0), jnp.moveaxis(a_max, 2, 0), jnp.moveaxis(b_last, 2, 0))
    _, (c_prev, n_prev, m_prev) = lax.scan(step, init, xs)
    c_prev = jnp.moveaxis(c_prev, 0, 2)
    n_prev = jnp.moveaxis(n_prev, 0, 2)
    m_prev = jnp.moveaxis(m_prev, 0, 2)
    inter_log = b + m_prev[..., None]
    m_t = jnp.maximum(inter_log, m_intra)
    s_inter = jnp.exp(inter_log - m_t)
    scores = jnp.einsum('bhctd,bhcsd->bhcts', q, k) * jnp.exp(d_log - m_t[..., None])
    num = jnp.einsum('bhcts,bhcsv->bhctv', scores, v) + s_inter[..., None] * jnp.einsum('bhctk,bhckv->bhctv', q, c_prev)
    den = jnp.sum(scores, axis=-1) + s_inter * jnp.einsum('bhctk,bhck->bhct', q, n_prev)
    h = num / jnp.maximum(jnp.abs(den), jnp.exp(-m_t))[..., None]
    return h.reshape(Bsz, H, L, Dh)


def _complex_affine_combine(e1, e2):
    a1r, a1i, b1r, b1i = e1
    a2r, a2i, b2r, b2i = e2
    return (a2r * a1r - a2i * a1i, a2r * a1i + a2i * a1r,
            a2r * b1r - a2i * b1i + b2r, a2r * b1i + a2i * b1r + b2i)


def s5_ssm(u, a_re, a_im, log_dt, b_re, b_im, c_re, c_im, d_skip):
    f32 = jnp.float32
    Bsz, L, _ = u.shape
    ug = u.astype(f32).reshape(Bsz, L, S_GROUPS, S_GROUP)
    a_re, a_im, b_re, b_im, c_re, c_im = (t.astype(f32) for t in (a_re, a_im, b_re, b_im, c_re, c_im))
    dt = jnp.exp(log_dt.astype(f32))[:, None]
    mag = jnp.exp(a_re * dt)
    ab_re = mag * jnp.cos(a_im * dt)
    ab_im = mag * jnp.sin(a_im * dt)
    inv = 1.0 / (a_re * a_re + a_im * a_im)
    g_re = ((ab_re - 1.0) * a_re + ab_im * a_im) * inv
    g_im = (ab_im * a_re - (ab_re - 1.0) * a_im) * inv
    bb_re = g_re[..., None] * b_re - g_im[..., None] * b_im
    bb_im = g_re[..., None] * b_im + g_im[..., None] * b_re
    bu_re = jnp.einsum('blgp,gnp->blgn', ug, bb_re)
    bu_im = jnp.einsum('blgp,gnp->blgn', ug, bb_im)
    shp = (1, L, S_GROUPS, S_STATE)
    a_seq_re = jnp.broadcast_to(ab_re, shp)
    a_seq_im = jnp.broadcast_to(ab_im, shp)
    _, _, x_re, x_im = lax.associative_scan(_complex_affine_combine, (a_seq_re, a_seq_im, bu_re, bu_im), axis=1)
    y = jnp.einsum('blgn,gpn->blgp', x_re, c_re) - jnp.einsum('blgn,gpn->blgp', x_im, c_im)
    y = y + d_skip.astype(f32).reshape(S_GROUPS, S_GROUP) * ug
    return y.reshape(Bsz, L, S_WIDTH)


def mixer_ab(x, w_in, conv_w, conv_b, wq, wk, wv, w_if, b_if, mh_gain, skip,
             a_re, a_im, log_dt, b_re, b_im, c_re, c_im, d_skip, w_glu, b_glu, w_out):
    Bsz, L, _ = x.shape
    proj = x @ w_in
    xm = proj[..., :M_WIDTH]
    zm = proj[..., M_WIDTH:2 * M_WIDTH]
    us = proj[..., 2 * M_WIDTH:]
    xc = jax.nn.silu(causal_dwconv(xm, conv_w, conv_b))
    nb = M_WIDTH // QKV_BLOCK
    xc_b = xc.reshape(Bsz, L, nb, QKV_BLOCK)
    xm_b = xm.reshape(Bsz, L, nb, QKV_BLOCK)
    q = jnp.einsum('blnd,nde->blne', xc_b, wq).reshape(Bsz, L, M_WIDTH)
    k = jnp.einsum('blnd,nde->blne', xc_b, wk).reshape(Bsz, L, M_WIDTH) * (M_HEAD_DIM ** -0.5)
    v = jnp.einsum('blnd,nde->blne', xm_b, wv).reshape(Bsz, L, M_WIDTH)
    gates = jnp.concatenate([q, k, v], axis=-1) @ w_if + b_if
    to_h = lambda t: t.reshape(Bsz, L, M_HEADS, M_HEAD_DIM).transpose(0, 2, 1, 3)
    h = mlstm_chunkwise(to_h(q), to_h(k), to_h(v),
                        gates[..., :M_HEADS].transpose(0, 2, 1), gates[..., M_HEADS:].transpose(0, 2, 1))
    mu = jnp.mean(h, axis=-1, keepdims=True)
    var = jnp.mean(jnp.square(h - mu), axis=-1, keepdims=True)
    hn = ((h - mu) * lax.rsqrt(var + EPS)).transpose(0, 2, 1, 3).reshape(Bsz, L, M_WIDTH) * mh_gain
    out_m = (hn + skip * xc) * jax.nn.silu(zm)
    y = jax.nn.gelu(s5_ssm(us, a_re, a_im, log_dt, b_re, b_im, c_re, c_im, d_skip))
    out_s = y * jax.nn.sigmoid(y @ w_glu + b_glu)
    return jnp.concatenate([out_m, out_s], axis=-1) @ w_out


def hgrn2_chunkwise(q, k, v, log_f):
    f32 = jnp.float32
    Bsz, H, L, E = q.shape
    C = H_CHUNK
    NC = L // C
    q, k, v, log_f = (t.astype(f32).reshape(Bsz, H, NC, C, E) for t in (q, k, v, log_f))
    b = jnp.cumsum(log_f, axis=3)
    b_last = b[:, :, :, -1:, :]
    q_inter = q * jnp.exp(b)
    k_state = k * jnp.exp(b_last - b)
    q_intra = q * jnp.exp(b - b_last)
    causal = jnp.tril(jnp.ones((C, C), dtype=bool))
    scores = jnp.where(causal, jnp.einsum('bhctk,bhcsk->bhcts', q_intra, k_state), 0.0)
    intra = jnp.einsum('bhcts,bhcsv->bhctv', scores, v)
    U = jnp.einsum('bhcsk,bhcsv->bhckv', k_state, v)
    decay = jnp.exp(b_last[:, :, :, 0, :])

    def step(s, inp):
        u_c, d_c = inp
        return d_c[..., None] * s + u_c, s

    _, s_prev = lax.scan(step, jnp.zeros((Bsz, H, E, E), f32), (jnp.moveaxis(U, 2, 0), jnp.moveaxis(decay, 2, 0)))
    s_prev = jnp.moveaxis(s_prev, 0, 2)
    inter = jnp.einsum('bhctk,bhckv->bhctv', q_inter, s_prev)
    return (intra + inter).reshape(Bsz, H, L, E)


def mixer_c(x, layer_idx, w_in, lb_raw, g_gain, w_out):
    Bsz, L, _ = x.shape
    proj = x @ w_in
    q = jax.nn.silu(proj[..., :H_WIDTH])
    f = proj[..., H_WIDTH:2 * H_WIDTH].astype(jnp.float32)
    i = proj[..., 2 * H_WIDTH:3 * H_WIDTH]
    g = proj[..., 3 * H_WIDTH:]
    lbs = jnp.cumsum(jax.nn.softmax(lb_raw.astype(jnp.float32), axis=0), axis=0)
    lb = lbs[layer_idx] - lbs[0]
    fg = lb + (1.0 - lb) * jax.nn.sigmoid(f)
    to_h = lambda t: t.reshape(Bsz, L, H_HEADS, H_EXPAND).transpose(0, 2, 1, 3)
    o = hgrn2_chunkwise(to_h(q), to_h(1.0 - fg), to_h(i), to_h(jnp.log(fg)))
    o = o * lax.rsqrt(jnp.mean(o * o, axis=-1, keepdims=True) + EPS)
    o = o.transpose(0, 2, 1, 3).reshape(Bsz, L, H_WIDTH) * g_gain * jax.nn.silu(g)
    return o @ w_out


def setup_inputs(seed: int = 0) -> dict:
    key = jax.random.key(seed)
    ks = iter(jax.random.split(key, 48))
    f32 = jnp.float32
    nrm = lambda shape, scale: scale * jax.random.normal(next(ks), shape, f32)
    NE, NO = N_EVEN, N_ODD
    nb = M_WIDTH // QKV_BLOCK
    b_if = jnp.concatenate([nrm((NE, M_HEADS), 0.1),
                            jnp.broadcast_to(jnp.linspace(3.0, 6.0, M_HEADS, dtype=f32), (NE, M_HEADS)) + nrm((NE, M_HEADS), 0.01)], axis=-1)
    return {
        'x': nrm((BATCH, SEQ, D_MODEL), 1.0),
        'norm_g': 1.0 + nrm((DEPTH, 4, D_MODEL), 0.05),
        'ab_w_in': nrm((NE, D_MODEL, IN_EVEN), D_MODEL ** -0.5),
        'ab_conv_w': nrm((NE, M_CONV, M_WIDTH), M_CONV ** -0.5),
        'ab_conv_b': nrm((NE, M_WIDTH), 0.01),
        'ab_wq': nrm((NE, nb, QKV_BLOCK, QKV_BLOCK), QKV_BLOCK ** -0.5),
        'ab_wk': nrm((NE, nb, QKV_BLOCK, QKV_BLOCK), QKV_BLOCK ** -0.5),
        'ab_wv': nrm((NE, nb, QKV_BLOCK, QKV_BLOCK), QKV_BLOCK ** -0.5),
        'ab_w_if': nrm((NE, 3 * M_WIDTH, 2 * M_HEADS), 0.1 * (3 * M_WIDTH) ** -0.5),
        'ab_b_if': b_if,
        'ab_mh_gain': 1.0 + nrm((NE, M_WIDTH), 0.05),
        'ab_skip': 1.0 + nrm((NE, M_WIDTH), 0.05),
        'ab_a_re': -0.5 + nrm((NE, S_GROUPS, S_STATE), 0.01),
        'ab_a_im': jnp.pi * jnp.arange(S_STATE, dtype=f32) + nrm((NE, S_GROUPS, S_STATE), 0.01),
        'ab_log_dt': jax.random.uniform(next(ks), (NE, S_GROUPS), f32, math.log(DT_MIN), math.log(DT_MAX)),
        'ab_b_re': nrm((NE, S_GROUPS, S_STATE, S_GROUP), (2 * S_GROUP) ** -0.5),
        'ab_b_im': nrm((NE, S_GROUPS, S_STATE, S_GROUP), (2 * S_GROUP) ** -0.5),
        'ab_c_re': nrm((NE, S_GROUPS, S_GROUP, S_STATE), S_STATE ** -0.5),
        'ab_c_im': nrm((NE, S_GROUPS, S_GROUP, S_STATE), S_STATE ** -0.5),
        'ab_d': nrm((NE, S_WIDTH), 1.0),
        'ab_w_glu': nrm((NE, S_WIDTH, S_WIDTH), S_WIDTH ** -0.5),
        'ab_b_glu': nrm((NE, S_WIDTH), 0.01),
        'ab_w_out': nrm((NE, MIX_EVEN, D_MODEL), MIX_EVEN ** -0.5),
        'c_w_in': nrm((NO, D_MODEL, IN_ODD), D_MODEL ** -0.5),
        'c_lb_raw': nrm((DEPTH, H_WIDTH), 0.1),
        'c_g_gain': 1.0 + nrm((NO, H_WIDTH), 0.05),
        'c_w_out': nrm((NO, H_WIDTH, D_MODEL), H_WIDTH ** -0.5),
        'ffn_w1': nrm((DEPTH, D_MODEL, D_FF), D_MODEL ** -0.5),
        'ffn_w3': nrm((DEPTH, D_MODEL, D_FF), D_MODEL ** -0.5),
        'ffn_w2': nrm((DEPTH, D_FF, D_MODEL), D_FF ** -0.5),
    }


def reference(x, norm_g, ab_w_in, ab_conv_w, ab_conv_b, ab_wq, ab_wk, ab_wv, ab_w_if, ab_b_if, ab_mh_gain, ab_skip,
              ab_a_re, ab_a_im, ab_log_dt, ab_b_re, ab_b_im, ab_c_re, ab_c_im, ab_d, ab_w_glu, ab_b_glu, ab_w_out,
              c_w_in, c_lb_raw, c_g_gain, c_w_out, ffn_w1, ffn_w3, ffn_w2):
    for layer in range(DEPTH):
        j = layer // 2
        h = rmsnorm(x, norm_g[layer, 0])
        if layer % 2 == 0:
            mix = mixer_ab(h, ab_w_in[j], ab_conv_w[j], ab_conv_b[j], ab_wq[j], ab_wk[j], ab_wv[j], ab_w_if[j],
                           ab_b_if[j], ab_mh_gain[j], ab_skip[j], ab_a_re[j], ab_a_im[j], ab_log_dt[j], ab_b_re[j],
                           ab_b_im[j], ab_c_re[j], ab_c_im[j], ab_d[j], ab_w_glu[j], ab_b_glu[j], ab_w_out[j])
        else:
            mix = mixer_c(h, layer, c_w_in[j], c_lb_raw, c_g_gain[j], c_w_out[j])
        x = x + rmsnorm(mix, norm_g[layer, 1])
        h = rmsnorm(x, norm_g[layer, 2])
        x = x + rmsnorm(swiglu(h, ffn_w1[layer], ffn_w3[layer], ffn_w2[layer]), norm_g[layer, 3])
    return x
```

```python
import functools
import math

import jax
import jax.numpy as jnp
from jax import lax
from jax.experimental import pallas as pl
from jax.experimental.pallas import tpu as pltpu

F32 = jnp.float32
BF16 = jnp.bfloat16
EPS = 1e-6

V7X_VMEM_LIMIT_BYTES = 56 * 1024 * 1024
SUBLANES = 8
LANES = 128
MXU_TILE = 256

M_HEADS = 4
M_CONV = 4
QKV_BLOCK = 4
S_GROUP = 16
S_STATE = 64
H_EXPAND = 128

ROW_TILE_PROJ = 512
ROW_TILE_TAIL = 256
MLSTM_CHUNK = 256
S5_TILE = 256
S5_TILES_PER_PASS = 8
HGRN_TILE = 256
HGRN_CHUNK = 64


def _dot(a, b):
    return jnp.dot(a, b, preferred_element_type=F32)


def _dot_nt(a, b):
    return lax.dot_general(a, b, (((1,), (1,)), ((), ())), preferred_element_type=F32)


def _dot_tn(a, b):
    return lax.dot_general(a, b, (((0,), (0,)), ((), ())), preferred_element_type=F32)


def _sigmoid(x):
    return 1.0 / (1.0 + jnp.exp(-x))


def _silu(x):
    return x * _sigmoid(x)


def _log_sigmoid(x):
    return jnp.minimum(x, 0.0) - jnp.log(1.0 + jnp.exp(-jnp.abs(x)))


def _gelu_tanh(x):
    return 0.5 * x * (1.0 + jnp.tanh(math.sqrt(2.0 / math.pi) * (x + 0.044715 * (x * x * x))))


def _rms(x, g):
    return x * lax.rsqrt(jnp.mean(x * x, axis=-1, keepdims=True) + EPS) * g


def _cumsum_rows(x):
    rows, n = x.shape
    row = lax.broadcasted_iota(jnp.int32, (SUBLANES, n), 0)
    outs = []
    carry = None
    for r in range(rows // SUBLANES):
        t = x[r * SUBLANES:(r + 1) * SUBLANES, :]
        for k in (1, 2, 4):
            t = t + jnp.where(row >= k, pltpu.roll(t, k, axis=0), 0.0)
        if carry is not None:
            t = t + carry
        carry = t[SUBLANES - 1:SUBLANES, :]
        outs.append(t)
    return jnp.concatenate(outs, axis=0)


def _params(n_axes):
    return pltpu.CompilerParams(dimension_semantics=("arbitrary",) * n_axes,
                                vmem_limit_bytes=V7X_VMEM_LIMIT_BYTES)


def _const_spec(shape):
    nd = len(shape)
    return pl.BlockSpec(shape, lambda *_: (0,) * nd)


def _rms_matmul_kernel(x_ref, g_ref, w_ref, o_ref):
    h = _rms(x_ref[...], g_ref[...])
    o_ref[...] = _dot(h.astype(BF16), w_ref[...]).astype(o_ref.dtype)


def _rms_matmul(x2d, g, w, tm):
    t, d = x2d.shape
    n = w.shape[1]
    return pl.pallas_call(
        _rms_matmul_kernel,
        grid=(t // tm,),
        in_specs=[pl.BlockSpec((tm, d), lambda i: (i, 0)), _const_spec((1, d)), _const_spec((d, n))],
        out_specs=pl.BlockSpec((tm, n), lambda i: (i, 0)),
        out_shape=jax.ShapeDtypeStruct((t, n), F32),
        compiler_params=_params(1),
        name="rms_proj",
    )(x2d, g.reshape(1, d), w)


def _mlstm_kernel(p_ref, convw_ref, convb_ref, wq_ref, wk_ref, wv_ref, wif_ref, bif_ref, gain_ref, skip_ref,
                  o_ref, xbuf, c_s, n_s, m_s, *, heads):
    chunk = o_ref.shape[1]
    width = o_ref.shape[2]
    hd = width // heads
    taps = convw_ref.shape[0]

    @pl.when(pl.program_id(1) == 0)
    def _init():
        xbuf[0:SUBLANES, :] = jnp.zeros((SUBLANES, width), F32)
        c_s[...] = jnp.zeros_like(c_s)
        n_s[...] = jnp.zeros_like(n_s)
        m_s[...] = jnp.zeros_like(m_s)

    xm = p_ref[0, :, 0:width]
    zm = p_ref[0, :, width:2 * width]

    xbuf[SUBLANES:SUBLANES + chunk, :] = xm
    conv = convb_ref[...] + convw_ref[taps - 1:taps, :] * xm
    for k in range(taps - 1):
        conv = conv + convw_ref[k:k + 1, :] * xbuf[pl.ds(SUBLANES - (taps - 1) + k, chunk), :]
    xbuf[0:SUBLANES, :] = xm[chunk - SUBLANES:chunk, :]
    xc = _silu(conv)

    xc_b = xc.astype(BF16)
    q = _dot(xc_b, wq_ref[...])
    k = _dot(xc_b, wk_ref[...]) * (hd ** -0.5)
    v = _dot(xm.astype(BF16), wv_ref[...])
    q_b, k_b, v_b = q.astype(BF16), k.astype(BF16), v.astype(BF16)
    gates = (_dot(q_b, wif_ref[0:width, :]) + _dot(k_b, wif_ref[width:2 * width, :])
             + _dot(v_b, wif_ref[2 * width:3 * width, :]) + bif_ref[...])

    lane = lax.broadcasted_iota(jnp.int32, gates.shape, 1)
    bcum = _cumsum_rows(_log_sigmoid(gates))
    cols = jnp.where(lane < heads, gates, bcum)
    rows = cols.T
    ti = lax.broadcasted_iota(jnp.int32, (chunk, chunk), 0)
    si = lax.broadcasted_iota(jnp.int32, (chunk, chunk), 1)
    causal = si <= ti

    for h in range(heads):
        sl = slice(h * hd, (h + 1) * hd)
        qh, kh = q[:, sl], k[:, sl]
        qh_b, kh_b, vh_b = q_b[:, sl], k_b[:, sl], v_b[:, sl]
        li_col = cols[:, h:h + 1]
        b_col = cols[:, heads + h:heads + h + 1]
        li_row = rows[h:h + 1, :]
        b_row = rows[heads + h:heads + h + 1, :]
        b_last = cols[chunk - 1:chunk, heads + h:heads + h + 1]
        m_prev = m_s[h, 0:1, 0:1]
        c_prev = c_s[h]
        n_prev = n_s[h, 0:1, :]

        d_log = jnp.where(causal, b_col - b_row + li_row, -jnp.inf)
        m_intra = jnp.max(d_log, axis=-1, keepdims=True)
        m_t = jnp.maximum(b_col + m_prev, m_intra)
        scores = _dot_nt(qh_b, kh_b) * jnp.exp(d_log - m_t)
        s_inter = jnp.exp(b_col + m_prev - m_t)
        num = _dot(scores.astype(BF16), vh_b) + s_inter * _dot(qh_b, c_prev.astype(BF16))
        den = (jnp.sum(scores, axis=-1, keepdims=True)
               + s_inter * jnp.sum(qh * n_prev, axis=-1, keepdims=True))
        hh = num / jnp.maximum(jnp.abs(den), jnp.exp(-m_t))

        a_col = b_last - b_col + li_col
        a_row = b_last - b_row + li_row
        a_max = jnp.max(a_row, axis=-1, keepdims=True)
        kw = kh * jnp.exp(a_col - a_max)
        u = _dot_tn(kw.astype(BF16), vh_b)
        un = jnp.sum(kw, axis=0, keepdims=True)
        m_new = jnp.maximum(b_last + m_prev, a_max)
        s_old = jnp.exp(b_last + m_prev - m_new)
        s_new = jnp.exp(a_max - m_new)
        c_s[h] = s_old * c_prev + s_new * u
        n_s[h] = jnp.broadcast_to(s_old * n_prev + s_new * un, (SUBLANES, hd))
        m_s[h] = jnp.broadcast_to(m_new, (SUBLANES, LANES))

        mu = jnp.mean(hh, axis=-1, keepdims=True)
        dev = hh - mu
        var = jnp.mean(dev * dev, axis=-1, keepdims=True)
        hn = dev * lax.rsqrt(var + EPS) * gain_ref[:, sl]
        o_ref[0, :, sl] = ((hn + skip_ref[:, sl] * xc[:, sl]) * _silu(zm[:, sl])).astype(o_ref.dtype)


def _mlstm(proj, conv_w, conv_b, wq, wk, wv, w_if, b_if, gain, skip, heads):
    b, l, _ = proj.shape
    width = conv_w.shape[1]
    hd = width // heads
    chunk = MLSTM_CHUNK
    eye = jnp.eye(width // QKV_BLOCK, dtype=F32)

    def block_diag(w):
        nb, d, e = w.shape
        return jnp.einsum('nde,nm->ndme', w, eye).reshape(nb * d, nb * e).astype(BF16)

    wif_pad = jnp.zeros((w_if.shape[0], LANES), F32).at[:, :2 * heads].set(w_if).astype(BF16)
    bif_pad = jnp.zeros((1, LANES), F32).at[0, :2 * heads].set(b_if)
    kern = functools.partial(_mlstm_kernel, heads=heads)
    return pl.pallas_call(
        kern,
        grid=(b, l // chunk),
        in_specs=[pl.BlockSpec((1, chunk, 2 * width), lambda bi, i: (bi, i, 0)),
                  _const_spec(conv_w.shape), _const_spec((1, width)),
                  _const_spec((width, width)), _const_spec((width, width)), _const_spec((width, width)),
                  _const_spec((3 * width, LANES)), _const_spec((1, LANES)),
                  _const_spec((1, width)), _const_spec((1, width))],
        out_specs=pl.BlockSpec((1, chunk, width), lambda bi, i: (bi, i, 0)),
        out_shape=jax.ShapeDtypeStruct((b, l, width), BF16),
        scratch_shapes=[pltpu.VMEM((chunk + SUBLANES, width), F32),
                        pltpu.VMEM((heads, hd, hd), F32),
                        pltpu.VMEM((heads, SUBLANES, hd), F32),
                        pltpu.VMEM((heads, SUBLANES, LANES), F32)],
        compiler_params=_params(2),
        name="mlstm",
    )(proj, conv_w, conv_b.reshape(1, width), block_diag(wq), block_diag(wk), block_diag(wv),
      wif_pad, bif_pad, gain.reshape(1, width), skip.reshape(1, width))


def _s5_kernel(u_ref, wb_ref, cst_ref, wc_ref, d_ref, wglu_ref, bglu_ref, o_ref, xs, carry):
    tm = o_ref.shape[1]
    width = o_ref.shape[2]
    n_tiles = wb_ref.shape[0]
    kt = wb_ref.shape[1]
    tiles_per_k = n_tiles // (width // kt)
    xw = 2 * LANES

    @pl.when(pl.program_id(1) == 0)
    def _init():
        carry[...] = jnp.zeros_like(carry)

    u = u_ref[0]
    u_b = u.astype(BF16)
    for j in range(n_tiles):
        kk = j // tiles_per_k
        xs[:, j * xw:(j + 1) * xw] = _dot(u_b[:, kk * kt:(kk + 1) * kt], wb_ref[j])

    for j0 in range(0, n_tiles, S5_TILES_PER_PASS):
        tiles = range(j0, min(j0 + S5_TILES_PER_PASS, n_tiles))

        def body(r, carr, tiles=tiles):
            r8 = pl.multiple_of(r * SUBLANES, SUBLANES)
            new = []
            for idx, j in enumerate(tiles):
                cr, ci = carr[2 * idx], carr[2 * idx + 1]
                xr = xs[pl.ds(r8, SUBLANES), j * xw:j * xw + LANES]
                xi = xs[pl.ds(r8, SUBLANES), j * xw + LANES:(j + 1) * xw]
                for lvl, k in enumerate((1, 2, 4)):
                    mr, mi = cst_ref[j, 2 * lvl], cst_ref[j, 2 * lvl + 1]
                    sr, si = pltpu.roll(xr, k, axis=0), pltpu.roll(xi, k, axis=0)
                    xr, xi = xr + (mr * sr - mi * si), xi + (mr * si + mi * sr)
                pr, pi = cst_ref[j, 6], cst_ref[j, 7]
                xr, xi = xr + (pr * cr - pi * ci), xi + (pr * ci + pi * cr)
                xs[pl.ds(r8, SUBLANES), j * xw:j * xw + LANES] = xr
                xs[pl.ds(r8, SUBLANES), j * xw + LANES:(j + 1) * xw] = xi
                new.append(jnp.broadcast_to(xr[SUBLANES - 1:SUBLANES, :], (SUBLANES, LANES)))
                new.append(jnp.broadcast_to(xi[SUBLANES - 1:SUBLANES, :], (SUBLANES, LANES)))
            return tuple(new)

        init = tuple(carry[j, c] for j in tiles for c in range(2))
        fin = lax.fori_loop(0, tm // SUBLANES, body, init)
        for idx, j in enumerate(tiles):
            carry[j, 0] = fin[2 * idx]
            carry[j, 1] = fin[2 * idx + 1]

    n_out = wc_ref.shape[0]
    kc = wc_ref.shape[1]
    y = jnp.concatenate([_dot(xs[:, h * kc:(h + 1) * kc].astype(BF16), wc_ref[h]) for h in range(n_out)], axis=1)
    y = _gelu_tanh(y + d_ref[...] * u)
    gate = _sigmoid(_dot(y.astype(BF16), wglu_ref[...]) + bglu_ref[...])
    o_ref[0] = (y * gate).astype(o_ref.dtype)


def _s5_weights(a_re, a_im, log_dt, b_re, b_im, c_re, c_im):
    groups, n_state, p = b_re.shape
    width = groups * p
    gpt = LANES // n_state
    n_tiles = groups // gpt
    dt = jnp.exp(log_dt)[:, None]
    mag = jnp.exp(a_re * dt)
    ab_re = mag * jnp.cos(a_im * dt)
    ab_im = mag * jnp.sin(a_im * dt)
    inv = 1.0 / (a_re * a_re + a_im * a_im)
    g_re = ((ab_re - 1.0) * a_re + ab_im * a_im) * inv
    g_im = (ab_im * a_re - (ab_re - 1.0) * a_im) * inv
    bb_re = g_re[..., None] * b_re - g_im[..., None] * b_im
    bb_im = g_re[..., None] * b_im + g_im[..., None] * b_re
    eye = jnp.eye(groups, dtype=F32)

    bb = jnp.stack([bb_re, bb_im], axis=0)
    wfull = jnp.einsum('cgnp,gh->gphcn', bb, eye)
    wfull = wfull.reshape(groups, p, n_tiles, gpt, 2, n_state).transpose(0, 1, 2, 4, 3, 5)
    wfull = wfull.reshape(width, n_tiles, 2 * LANES)
    kt = min(MXU_TILE, width)
    tiles_per_k = n_tiles // (width // kt)
    wb = jnp.stack([wfull[(j // tiles_per_k) * kt:(j // tiles_per_k + 1) * kt, j, :] for j in range(n_tiles)])

    cc = jnp.stack([c_re, -c_im], axis=0)
    cfull = jnp.einsum('cgpn,gh->hcngp', cc, eye)
    cfull = cfull.reshape(n_tiles, gpt, 2, n_state, width).transpose(0, 2, 1, 3, 4)
    cfull = cfull.reshape(n_tiles * 2 * LANES, width)
    n_out = width // kt
    kc = cfull.shape[0] // n_out
    wc = jnp.stack([cfull[h * kc:(h + 1) * kc, h * kt:(h + 1) * kt] for h in range(n_out)])

    a1 = (ab_re.reshape(n_tiles, LANES), ab_im.reshape(n_tiles, LANES))

    def cmul(x, y):
        return x[0] * y[0] - x[1] * y[1], x[0] * y[1] + x[1] * y[0]

    pows = [a1]
    for _ in range(SUBLANES - 1):
        pows.append(cmul(pows[-1], a1))
    row = jnp.arange(SUBLANES)[None, :, None]
    planes = []
    for k in (1, 2, 4):
        for c in range(2):
            planes.append(jnp.where(row >= k, pows[k - 1][c][:, None, :], 0.0))
    for c in range(2):
        planes.append(jnp.stack([pw[c] for pw in pows], axis=1))
    cst = jnp.stack(planes, axis=1).astype(F32)
    return wb.astype(BF16), cst, wc.astype(BF16)


def _s5(proj, col_block, a_re, a_im, log_dt, b_re, b_im, c_re, c_im, d_skip, w_glu, b_glu):
    b, l, _ = proj.shape
    width = d_skip.shape[0]
    wb, cst, wc = _s5_weights(a_re, a_im, log_dt, b_re, b_im, c_re, c_im)
    n_tiles = wb.shape[0]
    tm = S5_TILE
    return pl.pallas_call(
        _s5_kernel,
        grid=(b, l // tm),
        in_specs=[pl.BlockSpec((1, tm, width), lambda bi, i: (bi, i, col_block)),
                  _const_spec(wb.shape), _const_spec(cst.shape), _const_spec(wc.shape),
                  _const_spec((1, width)), _const_spec((width, width)), _const_spec((1, width))],
        out_specs=pl.BlockSpec((1, tm, width), lambda bi, i: (bi, i, 0)),
        out_shape=jax.ShapeDtypeStruct((b, l, width), BF16),
        scratch_shapes=[pltpu.VMEM((tm, n_tiles * 2 * LANES), F32),
                        pltpu.VMEM((n_tiles, 2, SUBLANES, LANES), F32)],
        compiler_params=_params(2),
        name="s5",
    )(proj, wb, cst, wc, d_skip.reshape(1, width), w_glu.astype(BF16), b_glu.reshape(1, width))


def _hgrn2_kernel(p_ref, lb_ref, gain_ref, o_ref, st, *, heads, chunk):
    tm = o_ref.shape[1]
    width = o_ref.shape[2]
    e = width // heads

    @pl.when(pl.program_id(1) == 0)
    def _init():
        st[...] = jnp.zeros_like(st)

    lb = lb_ref[...]
    ti = lax.broadcasted_iota(jnp.int32, (chunk, chunk), 0)
    si = lax.broadcasted_iota(jnp.int32, (chunk, chunk), 1)
    causal = si <= ti

    def body(c, _):
        r0 = pl.multiple_of(c * chunk, chunk)
        q = _silu(p_ref[0, pl.ds(r0, chunk), 0:width])
        f = p_ref[0, pl.ds(r0, chunk), width:2 * width]
        v_b = p_ref[0, pl.ds(r0, chunk), 2 * width:3 * width].astype(BF16)
        g = p_ref[0, pl.ds(r0, chunk), 3 * width:4 * width]
        fg = lb + (1.0 - lb) * _sigmoid(f)
        k = 1.0 - fg
        b = _cumsum_rows(jnp.log(fg))
        b_last = b[chunk - 1:chunk, :]
        q_inter = (q * jnp.exp(b)).astype(BF16)
        k_state = (k * jnp.exp(b_last - b)).astype(BF16)
        q_intra = (q * jnp.exp(b - b_last)).astype(BF16)
        decay = jnp.exp(b_last)
        gate = gain_ref[...] * _silu(g)
        for h in range(heads):
            sl = slice(h * e, (h + 1) * e)
            s_t = st[h]
            scores = jnp.where(causal, _dot_nt(q_intra[:, sl], k_state[:, sl]), 0.0)
            o = _dot(scores.astype(BF16), v_b[:, sl]) + _dot_nt(q_inter[:, sl], s_t.astype(BF16))
            st[h] = s_t * decay[:, sl] + _dot_tn(v_b[:, sl], k_state[:, sl])
            o = o * lax.rsqrt(jnp.mean(o * o, axis=-1, keepdims=True) + EPS)
            o_ref[0, pl.ds(r0, chunk), sl] = (o * gate[:, sl]).astype(o_ref.dtype)
        return 0

    lax.fori_loop(0, tm // chunk, body, 0)


def _hgrn2(proj, lb, gain):
    b, l, w4 = proj.shape
    width = w4 // 4
    heads = width // H_EXPAND
    tm = HGRN_TILE
    kern = functools.partial(_hgrn2_kernel, heads=heads, chunk=HGRN_CHUNK)
    return pl.pallas_call(
        kern,
        grid=(b, l // tm),
        in_specs=[pl.BlockSpec((1, tm, w4), lambda bi, i: (bi, i, 0)),
                  _const_spec((1, width)), _const_spec((1, width))],
        out_specs=pl.BlockSpec((1, tm, width), lambda bi, i: (bi, i, 0)),
        out_shape=jax.ShapeDtypeStruct((b, l, width), BF16),
        scratch_shapes=[pltpu.VMEM((heads, H_EXPAND, H_EXPAND), F32)],
        compiler_params=_params(2),
        name="hgrn2",
    )(proj, lb.reshape(1, width), gain.reshape(1, width))


def _tail_kernel(*refs, n_mix):
    x_ref = refs[0]
    mix_refs = refs[1:1 + n_mix]
    wo_refs = refs[1 + n_mix:1 + 2 * n_mix]
    g1_ref, g2_ref, g3_ref, w1_ref, w3_ref, w2_ref, o_ref = refs[1 + 2 * n_mix:]
    mp = _dot(mix_refs[0][...], wo_refs[0][...])
    for m_ref, w_ref in zip(mix_refs[1:], wo_refs[1:]):
        mp = mp + _dot(m_ref[...], w_ref[...])
    x1 = x_ref[...] + _rms(mp, g1_ref[...])
    h = _rms(x1, g2_ref[...]).astype(BF16)
    a = _dot(h, w1_ref[...])
    z = (_silu(a) * _dot(h, w3_ref[...])).astype(BF16)
    y = _dot(z, w2_ref[...])
    o_ref[...] = x1 + _rms(y, g3_ref[...])


def _block_tail(x2d, mixes, w_outs, g1, g2, g3, w1, w3, w2):
    t, d = x2d.shape
    f = w1.shape[1]
    tm = ROW_TILE_TAIL
    n_mix = len(mixes)
    row = lambda i: (i, 0)
    in_specs = ([pl.BlockSpec((tm, d), row)]
                + [pl.BlockSpec((tm, m.shape[1]), row) for m in mixes]
                + [_const_spec(w.shape) for w in w_outs]
                + [_const_spec((1, d))] * 3
                + [_const_spec((d, f)), _const_spec((d, f)), _const_spec((f, d))])
    return pl.pallas_call(
        functools.partial(_tail_kernel, n_mix=n_mix),
        grid=(t // tm,),
        in_specs=in_specs,
        out_specs=pl.BlockSpec((tm, d), row),
        out_shape=jax.ShapeDtypeStruct((t, d), F32),
        compiler_params=_params(1),
        name="outproj_swiglu",
    )(x2d, *mixes, *[w.astype(BF16) for w in w_outs], g1.reshape(1, d), g2.reshape(1, d), g3.reshape(1, d),
      w1.astype(BF16), w3.astype(BF16), w2.astype(BF16))


def kernel(x, norm_g, ab_w_in, ab_conv_w, ab_conv_b, ab_wq, ab_wk, ab_wv, ab_w_if, ab_b_if, ab_mh_gain, ab_skip,
           ab_a_re, ab_a_im, ab_log_dt, ab_b_re, ab_b_im, ab_c_re, ab_c_im, ab_d, ab_w_glu, ab_b_glu, ab_w_out,
           c_w_in, c_lb_raw, c_g_gain, c_w_out, ffn_w1, ffn_w3, ffn_w2):
    bsz, seq, d = x.shape
    t = bsz * seq
    depth = norm_g.shape[0]
    x2d = x.reshape(t, d)
    lbs = jnp.cumsum(jax.nn.softmax(c_lb_raw.astype(F32), axis=0), axis=0)
    for layer in range(depth):
        j = layer // 2
        if layer % 2 == 0:
            m_width = ab_conv_w.shape[2]
            s_width = ab_d.shape[1]
            proj = _rms_matmul(x2d, norm_g[layer, 0], ab_w_in[j].astype(BF16), ROW_TILE_PROJ).reshape(bsz, seq, -1)
            out_m = _mlstm(proj, ab_conv_w[j], ab_conv_b[j], ab_wq[j], ab_wk[j], ab_wv[j], ab_w_if[j], ab_b_if[j],
                           ab_mh_gain[j], ab_skip[j], M_HEADS)
            out_s = _s5(proj, (2 * m_width) // s_width, ab_a_re[j], ab_a_im[j], ab_log_dt[j], ab_b_re[j], ab_b_im[j],
                        ab_c_re[j], ab_c_im[j], ab_d[j], ab_w_glu[j], ab_b_glu[j])
            mixes = [out_m.reshape(t, m_width), out_s.reshape(t, s_width)]
            w_outs = [ab_w_out[j][:m_width], ab_w_out[j][m_width:]]
        else:
            proj = _rms_matmul(x2d, norm_g[layer, 0], c_w_in[j].astype(BF16), ROW_TILE_TAIL).reshape(bsz, seq, -1)
            lb = lbs[layer] - lbs[0]
            mixes = [_hgrn2(proj, lb, c_g_gain[j]).reshape(t, -1)]
            w_outs = [c_w_out[j]]
        x2d = _block_tail(x2d, mixes, w_outs, norm_g[layer, 1], norm_g[layer, 2], norm_g[layer, 3],
                          ffn_w1[layer], ffn_w3[layer], ffn_w2[layer])
    return x2d.reshape(bsz, seq, d)
```

```python
import functools
import math

import jax
import jax.numpy as jnp
from jax import lax
from jax.experimental import pallas as pl
from jax.experimental.pallas import tpu as pltpu

F32 = jnp.float32
BF16 = jnp.bfloat16
EPS = 1e-6

V7X_VMEM_LIMIT_BYTES = 56 * 1024 * 1024
SUBLANES = 8
LANES = 128
MXU_TILE = 256

M_HEADS = 4
M_CONV = 4
QKV_BLOCK = 4
S_GROUP = 16
S_STATE = 64
H_EXPAND = 128

ROW_TILE_PROJ = 512
ROW_TILE_TAIL = 512
MLSTM_CHUNK = 256
S5_TILE = 256
S5_TILES_PER_PASS = 8
S5_SCAN_UNROLL = 4
HGRN_TILE = 256
HGRN_CHUNK = 64


def _dot(a, b):
    return jnp.dot(a, b, preferred_element_type=F32)


def _dot_nt(a, b):
    return lax.dot_general(a, b, (((1,), (1,)), ((), ())), preferred_element_type=F32)


def _dot_tn(a, b):
    return lax.dot_general(a, b, (((0,), (0,)), ((), ())), preferred_element_type=F32)


def _sigmoid(x):
    return 1.0 / (1.0 + jnp.exp(-x))


def _silu(x):
    return x * _sigmoid(x)


def _log_sigmoid(x):
    return jnp.minimum(x, 0.0) - jnp.log(1.0 + jnp.exp(-jnp.abs(x)))


def _gelu_tanh(x):
    return 0.5 * x * (1.0 + jnp.tanh(math.sqrt(2.0 / math.pi) * (x + 0.044715 * (x * x * x))))


def _rms(x, g):
    return x * lax.rsqrt(jnp.mean(x * x, axis=-1, keepdims=True) + EPS) * g


def _prefix_rows(x, op, identity):
    rows, n = x.shape
    row = lax.broadcasted_iota(jnp.int32, (SUBLANES, n), 0)
    outs = []
    carry = None
    for r in range(rows // SUBLANES):
        t = x[r * SUBLANES:(r + 1) * SUBLANES, :]
        for k in (1, 2, 4):
            t = op(t, jnp.where(row >= k, pltpu.roll(t, k, axis=0), identity))
        if carry is not None:
            t = op(t, carry)
        carry = t[SUBLANES - 1:SUBLANES, :]
        outs.append(t)
    return jnp.concatenate(outs, axis=0)


def _cumsum_rows(x):
    return _prefix_rows(x, jnp.add, 0.0)


def _cummax_rows(x):
    return _prefix_rows(x, jnp.maximum, -jnp.inf)


def _params(n_axes):
    return pltpu.CompilerParams(dimension_semantics=("arbitrary",) * n_axes,
                                vmem_limit_bytes=V7X_VMEM_LIMIT_BYTES)


def _const_spec(shape):
    nd = len(shape)
    return pl.BlockSpec(shape, lambda *_: (0,) * nd, pipeline_mode=pl.Buffered(1))


def _rms_matmul_kernel(x_ref, g_ref, w_ref, o_ref):
    h = _rms(x_ref[...], g_ref[...])
    o_ref[...] = _dot(h.astype(BF16), w_ref[...]).astype(o_ref.dtype)


def _rms_matmul(x2d, g, w, tm):
    t, d = x2d.shape
    n = w.shape[1]
    return pl.pallas_call(
        _rms_matmul_kernel,
        grid=(t // tm,),
        in_specs=[pl.BlockSpec((tm, d), lambda i: (i, 0)), _const_spec((1, d)), _const_spec((d, n))],
        out_specs=pl.BlockSpec((tm, n), lambda i: (i, 0)),
        out_shape=jax.ShapeDtypeStruct((t, n), F32),
        compiler_params=_params(1),
        name="rms_proj",
    )(x2d, g.reshape(1, d), w)


def _mlstm_kernel(p_ref, convw_ref, convb_ref, wq_ref, wk_ref, wv_ref, wif_ref, bif_ref, gain_ref, skip_ref,
                  o_ref, xbuf, c_s, m_s, *, heads):
    chunk = o_ref.shape[1]
    width = o_ref.shape[2]
    hd = width // heads
    taps = convw_ref.shape[0]

    @pl.when(pl.program_id(1) == 0)
    def _init():
        xbuf[0:SUBLANES, :] = jnp.zeros((SUBLANES, width), F32)
        c_s[...] = jnp.zeros_like(c_s)
        m_s[...] = jnp.zeros_like(m_s)

    xm = p_ref[0, :, 0:width]
    zm = p_ref[0, :, width:2 * width]

    xbuf[SUBLANES:SUBLANES + chunk, :] = xm
    conv = convb_ref[...] + convw_ref[taps - 1:taps, :] * xm
    for k in range(taps - 1):
        conv = conv + convw_ref[k:k + 1, :] * xbuf[pl.ds(SUBLANES - (taps - 1) + k, chunk), :]
    xbuf[0:SUBLANES, :] = xm[chunk - SUBLANES:chunk, :]
    xc = _silu(conv)

    xc_b = xc.astype(BF16)
    q = _dot(xc_b, wq_ref[...])
    k = _dot(xc_b, wk_ref[...]) * (hd ** -0.5)
    v = _dot(xm.astype(BF16), wv_ref[...])
    q_b, k_b, v_b = q.astype(BF16), k.astype(BF16), v.astype(BF16)
    gates = (_dot(q_b, wif_ref[0:width, :]) + _dot(k_b, wif_ref[width:2 * width, :])
             + _dot(v_b, wif_ref[2 * width:3 * width, :]) + bif_ref[...])

    log_i = gates[:, 0:LANES]
    b = _cumsum_rows(_log_sigmoid(gates[:, LANES:2 * LANES]))
    g = log_i - b
    g_max = _cummax_rows(g)
    m_prev = m_s[0:1, :]
    m_t = jnp.maximum(b + m_prev, b + g_max)
    row_term = b - m_t
    e_neg_m = jnp.exp(-m_t)
    b_last = b[chunk - 1:chunk, :]
    a_max = b_last + g_max[chunk - 1:chunk, :]
    k_scale = jnp.exp(g - g_max[chunk - 1:chunk, :])
    m_new = jnp.maximum(b_last + m_prev, a_max)
    s_old = jnp.exp(b_last + m_prev - m_new)
    s_new = jnp.exp(a_max - m_new)
    m_s[...] = jnp.broadcast_to(m_new, m_s.shape)
    g_rows = g.T

    ti = lax.broadcasted_iota(jnp.int32, (chunk, chunk), 0)
    si = lax.broadcasted_iota(jnp.int32, (chunk, chunk), 1)
    causal = si <= ti
    ones_b = jnp.ones((chunk, hd), BF16)

    for h in range(heads):
        sl = slice(h * hd, (h + 1) * hd)
        kh = k[:, sl]
        qh_b, kh_b = q_b[:, sl], k_b[:, sl]
        v_ones = jnp.concatenate([v_b[:, sl], ones_b], axis=1)
        row_b = jnp.broadcast_to(row_term[:, h:h + 1], (chunk, hd))
        row_w = jnp.concatenate([row_b] * (chunk // hd), axis=1)
        w_intra = jnp.exp(jnp.where(causal, row_w + g_rows[h:h + 1, :], -jnp.inf))
        s_inter = jnp.exp(row_b + m_prev[:, h:h + 1])
        scores = (_dot_nt(qh_b, kh_b) * w_intra).astype(BF16)
        c_prev = c_s[h]
        tot = _dot(scores, v_ones) + jnp.concatenate([s_inter, s_inter], axis=1) * _dot(qh_b, c_prev.astype(BF16))
        den_floor = jnp.broadcast_to(e_neg_m[:, h:h + 1], (chunk, hd))
        hh = tot[:, 0:hd] / jnp.maximum(jnp.abs(tot[:, hd:2 * hd]), den_floor)

        kw = (kh * jnp.broadcast_to(k_scale[:, h:h + 1], (chunk, hd))).astype(BF16)
        c_s[h] = s_old[:, h:h + 1] * c_prev + s_new[:, h:h + 1] * _dot_tn(kw, v_ones)

        mu = jnp.mean(hh, axis=-1, keepdims=True)
        dev = hh - mu
        var = jnp.mean(dev * dev, axis=-1, keepdims=True)
        hn = dev * lax.rsqrt(var + EPS) * gain_ref[:, sl]
        o_ref[0, :, sl] = ((hn + skip_ref[:, sl] * xc[:, sl]) * _silu(zm[:, sl])).astype(o_ref.dtype)


def _mlstm(proj, conv_w, conv_b, wq, wk, wv, w_if, b_if, gain, skip, heads):
    b, l, _ = proj.shape
    width = conv_w.shape[1]
    hd = width // heads
    chunk = MLSTM_CHUNK
    eye = jnp.eye(width // QKV_BLOCK, dtype=F32)

    def block_diag(w):
        nb, d, e = w.shape
        return jnp.einsum('nde,nm->ndme', w, eye).reshape(nb * d, nb * e).astype(BF16)

    wif_pad = (jnp.zeros((w_if.shape[0], 2 * LANES), F32).at[:, :heads].set(w_if[:, :heads])
               .at[:, LANES:LANES + heads].set(w_if[:, heads:])).astype(BF16)
    bif_pad = (jnp.zeros((1, 2 * LANES), F32).at[0, :heads].set(b_if[:heads])
               .at[0, LANES:LANES + heads].set(b_if[heads:]))
    kern = functools.partial(_mlstm_kernel, heads=heads)
    return pl.pallas_call(
        kern,
        grid=(b, l // chunk),
        in_specs=[pl.BlockSpec((1, chunk, 2 * width), lambda bi, i: (bi, i, 0)),
                  _const_spec(conv_w.shape), _const_spec((1, width)),
                  _const_spec((width, width)), _const_spec((width, width)), _const_spec((width, width)),
                  _const_spec((3 * width, 2 * LANES)), _const_spec((1, 2 * LANES)),
                  _const_spec((1, width)), _const_spec((1, width))],
        out_specs=pl.BlockSpec((1, chunk, width), lambda bi, i: (bi, i, 0)),
        out_shape=jax.ShapeDtypeStruct((b, l, width), BF16),
        scratch_shapes=[pltpu.VMEM((chunk + SUBLANES, width), F32),
                        pltpu.VMEM((heads, hd, 2 * hd), F32),
                        pltpu.VMEM((SUBLANES, LANES), F32)],
        compiler_params=_params(2),
        name="mlstm",
    )(proj, conv_w, conv_b.reshape(1, width), block_diag(wq), block_diag(wk), block_diag(wv),
      wif_pad, bif_pad, gain.reshape(1, width), skip.reshape(1, width))


def _permute_rows(pm, x):
    hi = x.astype(BF16)
    rest = x - hi.astype(F32)
    mid = rest.astype(BF16)
    lo = (rest - mid.astype(F32)).astype(BF16)
    return (_dot(pm, hi) + _dot(pm, mid)) + _dot(pm, lo)


def _s5_kernel(u_ref, pm_ref, pmt_ref, wb_ref, cst_ref, wc_ref, d_ref, wglu_ref, bglu_ref, o_ref, xs, carry):
    tm = o_ref.shape[1]
    width = o_ref.shape[2]
    seg = tm // SUBLANES
    n_tiles = wb_ref.shape[0]
    kt = wb_ref.shape[1]
    tiles_per_k = n_tiles // (width // kt)
    xw = 2 * LANES

    @pl.when(pl.program_id(1) == 0)
    def _init():
        carry[...] = jnp.zeros_like(carry)

    u = _permute_rows(pm_ref[...], u_ref[0])
    u_b = u.astype(BF16)
    for j in range(n_tiles):
        kk = j // tiles_per_k
        xs[:, j * xw:(j + 1) * xw] = _dot(u_b[:, kk * kt:(kk + 1) * kt], wb_ref[j])

    row = lax.broadcasted_iota(jnp.int32, (SUBLANES, LANES), 0)

    def recur(t, state, tiles, store):
        r8 = pl.multiple_of(t * SUBLANES, SUBLANES)
        new = []
        for idx, j in enumerate(tiles):
            ar, ai = cst_ref[j, 0], cst_ref[j, 1]
            xr, xi = state[2 * idx], state[2 * idx + 1]
            br = xs[pl.ds(r8, SUBLANES), j * xw:j * xw + LANES]
            bi = xs[pl.ds(r8, SUBLANES), j * xw + LANES:(j + 1) * xw]
            xr, xi = (ar * xr - ai * xi) + br, (ar * xi + ai * xr) + bi
            if store:
                xs[pl.ds(r8, SUBLANES), j * xw:j * xw + LANES] = xr
                xs[pl.ds(r8, SUBLANES), j * xw + LANES:(j + 1) * xw] = xi
            new += [xr, xi]
        return tuple(new)

    for j0 in range(0, n_tiles, S5_TILES_PER_PASS):
        tiles = tuple(range(j0, min(j0 + S5_TILES_PER_PASS, n_tiles)))
        zeros = tuple(jnp.zeros((SUBLANES, LANES), F32) for _ in range(2 * len(tiles)))
        ends = lax.fori_loop(0, seg, functools.partial(recur, tiles=tiles, store=False), zeros,
                             unroll=S5_SCAN_UNROLL)

        starts = []
        for idx, j in enumerate(tiles):
            er, ei = ends[2 * idx], ends[2 * idx + 1]
            for lvl, k in enumerate((1, 2, 4)):
                mr, mi = cst_ref[j, 2 + 2 * lvl], cst_ref[j, 3 + 2 * lvl]
                sr, si = pltpu.roll(er, k, axis=0), pltpu.roll(ei, k, axis=0)
                er, ei = er + (mr * sr - mi * si), ei + (mr * si + mi * sr)
            pr, pi = cst_ref[j, 8], cst_ref[j, 9]
            cr, ci = carry[j, 0], carry[j, 1]
            er, ei = er + (pr * cr - pi * ci), ei + (pr * ci + pi * cr)
            carry[j, 0] = jnp.broadcast_to(er[SUBLANES - 1:SUBLANES, :], (SUBLANES, LANES))
            carry[j, 1] = jnp.broadcast_to(ei[SUBLANES - 1:SUBLANES, :], (SUBLANES, LANES))
            starts.append(jnp.where(row == 0, cr, pltpu.roll(er, 1, axis=0)))
            starts.append(jnp.where(row == 0, ci, pltpu.roll(ei, 1, axis=0)))
        lax.fori_loop(0, seg, functools.partial(recur, tiles=tiles, store=True), tuple(starts),
                      unroll=S5_SCAN_UNROLL)

    n_out = wc_ref.shape[0]
    kc = wc_ref.shape[1]
    y = jnp.concatenate([_dot(xs[:, h * kc:(h + 1) * kc].astype(BF16), wc_ref[h]) for h in range(n_out)], axis=1)
    y = _gelu_tanh(y + d_ref[...] * u)
    gate = _sigmoid(_dot(y.astype(BF16), wglu_ref[...]) + bglu_ref[...])
    o_ref[0] = _dot(pmt_ref[...], (y * gate).astype(BF16)).astype(o_ref.dtype)


def _s5_weights(a_re, a_im, log_dt, b_re, b_im, c_re, c_im, seg):
    groups, n_state, p = b_re.shape
    width = groups * p
    gpt = LANES // n_state
    n_tiles = groups // gpt
    dt = jnp.exp(log_dt)[:, None]
    mag = jnp.exp(a_re * dt)
    ab_re = mag * jnp.cos(a_im * dt)
    ab_im = mag * jnp.sin(a_im * dt)
    inv = 1.0 / (a_re * a_re + a_im * a_im)
    g_re = ((ab_re - 1.0) * a_re + ab_im * a_im) * inv
    g_im = (ab_im * a_re - (ab_re - 1.0) * a_im) * inv
    bb_re = g_re[..., None] * b_re - g_im[..., None] * b_im
    bb_im = g_re[..., None] * b_im + g_im[..., None] * b_re
    eye = jnp.eye(groups, dtype=F32)

    bb = jnp.stack([bb_re, bb_im], axis=0)
    wfull = jnp.einsum('cgnp,gh->gphcn', bb, eye)
    wfull = wfull.reshape(groups, p, n_tiles, gpt, 2, n_state).transpose(0, 1, 2, 4, 3, 5)
    wfull = wfull.reshape(width, n_tiles, 2 * LANES)
    kt = min(MXU_TILE, width)
    tiles_per_k = n_tiles // (width // kt)
    wb = jnp.stack([wfull[(j // tiles_per_k) * kt:(j // tiles_per_k + 1) * kt, j, :] for j in range(n_tiles)])

    cc = jnp.stack([c_re, -c_im], axis=0)
    cfull = jnp.einsum('cgpn,gh->hcngp', cc, eye)
    cfull = cfull.reshape(n_tiles, gpt, 2, n_state, width).transpose(0, 2, 1, 3, 4)
    cfull = cfull.reshape(n_tiles * 2 * LANES, width)
    n_out = width // kt
    kc = cfull.shape[0] // n_out
    wc = jnp.stack([cfull[h * kc:(h + 1) * kc, h * kt:(h + 1) * kt] for h in range(n_out)])

    a1 = (ab_re.reshape(n_tiles, LANES), ab_im.reshape(n_tiles, LANES))

    def cmul(x, y):
        return x[0] * y[0] - x[1] * y[1], x[0] * y[1] + x[1] * y[0]

    a_seg = a1
    assert seg & (seg - 1) == 0, "segment length must be a power of two"
    for _ in range(seg.bit_length() - 1):
        a_seg = cmul(a_seg, a_seg)
    pows = [a_seg]
    for _ in range(SUBLANES - 1):
        pows.append(cmul(pows[-1], a_seg))
    row = jnp.arange(SUBLANES)[None, :, None]
    planes = [jnp.broadcast_to(a1[c][:, None, :], (n_tiles, SUBLANES, LANES)) for c in range(2)]
    for k in (1, 2, 4):
        for c in range(2):
            planes.append(jnp.where(row >= k, pows[k - 1][c][:, None, :], 0.0))
    for c in range(2):
        planes.append(jnp.stack([pw[c] for pw in pows], axis=1))
    cst = jnp.stack(planes, axis=1).astype(F32)
    return wb.astype(BF16), cst, wc.astype(BF16)


def _s5(proj, col_block, a_re, a_im, log_dt, b_re, b_im, c_re, c_im, d_skip, w_glu, b_glu):
    b, l, _ = proj.shape
    width = d_skip.shape[0]
    tm = S5_TILE
    seg = tm // SUBLANES
    wb, cst, wc = _s5_weights(a_re, a_im, log_dt, b_re, b_im, c_re, c_im, seg)
    n_tiles = wb.shape[0]
    rp = jnp.arange(tm)
    pm = ((rp % SUBLANES) * seg + rp // SUBLANES)[:, None] == rp[None, :]
    return pl.pallas_call(
        _s5_kernel,
        grid=(b, l // tm),
        in_specs=[pl.BlockSpec((1, tm, width), lambda bi, i: (bi, i, col_block)),
                  _const_spec((tm, tm)), _const_spec((tm, tm)),
                  _const_spec(wb.shape), _const_spec(cst.shape), _const_spec(wc.shape),
                  _const_spec((1, width)), _const_spec((width, width)), _const_spec((1, width))],
        out_specs=pl.BlockSpec((1, tm, width), lambda bi, i: (bi, i, 0)),
        out_shape=jax.ShapeDtypeStruct((b, l, width), BF16),
        scratch_shapes=[pltpu.VMEM((tm, n_tiles * 2 * LANES), F32),
                        pltpu.VMEM((n_tiles, 2, SUBLANES, LANES), F32)],
        compiler_params=_params(2),
        name="s5",
    )(proj, pm.astype(BF16), pm.T.astype(BF16), wb, cst, wc, d_skip.reshape(1, width), w_glu.astype(BF16),
      b_glu.reshape(1, width))


def _hgrn2_kernel(p_ref, lb_ref, gain_ref, o_ref, st, *, heads, chunk):
    tm = o_ref.shape[1]
    width = o_ref.shape[2]
    e = width // heads

    @pl.when(pl.program_id(1) == 0)
    def _init():
        st[...] = jnp.zeros_like(st)

    lb = lb_ref[...]
    ti = lax.broadcasted_iota(jnp.int32, (chunk, chunk), 0)
    si = lax.broadcasted_iota(jnp.int32, (chunk, chunk), 1)
    causal = si <= ti

    def body(c, _):
        r0 = pl.multiple_of(c * chunk, chunk)
        q = _silu(p_ref[0, pl.ds(r0, chunk), 0:width])
        f = p_ref[0, pl.ds(r0, chunk), width:2 * width]
        v_b = p_ref[0, pl.ds(r0, chunk), 2 * width:3 * width].astype(BF16)
        g = p_ref[0, pl.ds(r0, chunk), 3 * width:4 * width]
        fg = lb + (1.0 - lb) * _sigmoid(f)
        k = 1.0 - fg
        b = _cumsum_rows(jnp.log(fg))
        b_last = b[chunk - 1:chunk, :]
        q_inter = (q * jnp.exp(b)).astype(BF16)
        k_state = (k * jnp.exp(b_last - b)).astype(BF16)
        q_intra = (q * jnp.exp(b - b_last)).astype(BF16)
        decay = jnp.exp(b_last)
        gate = gain_ref[...] * _silu(g)
        for h in range(heads):
            sl = slice(h * e, (h + 1) * e)
            s_t = st[h]
            scores = jnp.where(causal, _dot_nt(q_intra[:, sl], k_state[:, sl]), 0.0)
            o = _dot(scores.astype(BF16), v_b[:, sl]) + _dot_nt(q_inter[:, sl], s_t.astype(BF16))
            st[h] = s_t * decay[:, sl] + _dot_tn(v_b[:, sl], k_state[:, sl])
            o = o * lax.rsqrt(jnp.mean(o * o, axis=-1, keepdims=True) + EPS)
            o_ref[0, pl.ds(r0, chunk), sl] = (o * gate[:, sl]).astype(o_ref.dtype)
        return 0

    lax.fori_loop(0, tm // chunk, body, 0)


def _hgrn2(proj, lb, gain):
    b, l, w4 = proj.shape
    width = w4 // 4
    heads = width // H_EXPAND
    tm = HGRN_TILE
    kern = functools.partial(_hgrn2_kernel, heads=heads, chunk=HGRN_CHUNK)
    return pl.pallas_call(
        kern,
        grid=(b, l // tm),
        in_specs=[pl.BlockSpec((1, tm, w4), lambda bi, i: (bi, i, 0)),
                  _const_spec((1, width)), _const_spec((1, width))],
        out_specs=pl.BlockSpec((1, tm, width), lambda bi, i: (bi, i, 0)),
        out_shape=jax.ShapeDtypeStruct((b, l, width), BF16),
        scratch_shapes=[pltpu.VMEM((heads, H_EXPAND, H_EXPAND), F32)],
        compiler_params=_params(2),
        name="hgrn2",
    )(proj, lb.reshape(1, width), gain.reshape(1, width))


def _tail_kernel(*refs, n_mix):
    x_ref = refs[0]
    mix_refs = refs[1:1 + n_mix]
    wo_refs = refs[1 + n_mix:1 + 2 * n_mix]
    g1_ref, g2_ref, g3_ref, w1_ref, w3_ref, w2_ref, o_ref = refs[1 + 2 * n_mix:]
    mp = _dot(mix_refs[0][...], wo_refs[0][...])
    for m_ref, w_ref in zip(mix_refs[1:], wo_refs[1:]):
        mp = mp + _dot(m_ref[...], w_ref[...])
    x1 = x_ref[...] + _rms(mp, g1_ref[...])
    h = _rms(x1, g2_ref[...]).astype(BF16)
    a = _dot(h, w1_ref[...])
    z = (_silu(a) * _dot(h, w3_ref[...])).astype(BF16)
    y = _dot(z, w2_ref[...])
    o_ref[...] = x1 + _rms(y, g3_ref[...])


def _block_tail(x2d, mixes, w_outs, g1, g2, g3, w1, w3, w2):
    t, d = x2d.shape
    f = w1.shape[1]
    tm = ROW_TILE_TAIL
    n_mix = len(mixes)
    row = lambda i: (i, 0)
    in_specs = ([pl.BlockSpec((tm, d), row)]
                + [pl.BlockSpec((tm, m.shape[1]), row) for m in mixes]
                + [_const_spec(w.shape) for w in w_outs]
                + [_const_spec((1, d))] * 3
                + [_const_spec((d, f)), _const_spec((d, f)), _const_spec((f, d))])
    return pl.pallas_call(
        functools.partial(_tail_kernel, n_mix=n_mix),
        grid=(t // tm,),
        in_specs=in_specs,
        out_specs=pl.BlockSpec((tm, d), row),
        out_shape=jax.ShapeDtypeStruct((t, d), F32),
        compiler_params=_params(1),
        name="outproj_swiglu",
    )(x2d, *mixes, *[w.astype(BF16) for w in w_outs], g1.reshape(1, d), g2.reshape(1, d), g3.reshape(1, d),
      w1.astype(BF16), w3.astype(BF16), w2.astype(BF16))


def kernel(x, norm_g, ab_w_in, ab_conv_w, ab_conv_b, ab_wq, ab_wk, ab_wv, ab_w_if, ab_b_if, ab_mh_gain, ab_skip,
           ab_a_re, ab_a_im, ab_log_dt, ab_b_re, ab_b_im, ab_c_re, ab_c_im, ab_d, ab_w_glu, ab_b_glu, ab_w_out,
           c_w_in, c_lb_raw, c_g_gain, c_w_out, ffn_w1, ffn_w3, ffn_w2):
    bsz, seq, d = x.shape
    t = bsz * seq
    depth = norm_g.shape[0]
    x2d = x.reshape(t, d)
    lbs = jnp.cumsum(jax.nn.softmax(c_lb_raw.astype(F32), axis=0), axis=0)
    for layer in range(depth):
        j = layer // 2
        if layer % 2 == 0:
            m_width = ab_conv_w.shape[2]
            s_width = ab_d.shape[1]
            proj = _rms_matmul(x2d, norm_g[layer, 0], ab_w_in[j].astype(BF16), ROW_TILE_PROJ).reshape(bsz, seq, -1)
            out_m = _mlstm(proj, ab_conv_w[j], ab_conv_b[j], ab_wq[j], ab_wk[j], ab_wv[j], ab_w_if[j], ab_b_if[j],
                           ab_mh_gain[j], ab_skip[j], M_HEADS)
            out_s = _s5(proj, (2 * m_width) // s_width, ab_a_re[j], ab_a_im[j], ab_log_dt[j], ab_b_re[j], ab_b_im[j],
                        ab_c_re[j], ab_c_im[j], ab_d[j], ab_w_glu[j], ab_b_glu[j])
            mixes = [out_m.reshape(t, m_width), out_s.reshape(t, s_width)]
            w_outs = [ab_w_out[j][:m_width], ab_w_out[j][m_width:]]
        else:
            proj = _rms_matmul(x2d, norm_g[layer, 0], c_w_in[j].astype(BF16), ROW_TILE_PROJ).reshape(bsz, seq, -1)
            lb = lbs[layer] - lbs[0]
            mixes = [_hgrn2(proj, lb, c_g_gain[j]).reshape(t, -1)]
            w_outs = [c_w_out[j]]
        x2d = _block_tail(x2d, mixes, w_outs, norm_g[layer, 1], norm_g[layer, 2], norm_g[layer, 3],
                          ffn_w1[layer], ffn_w3[layer], ffn_w2[layer])
    return x2d.reshape(bsz, seq, d)
```

```python
import functools
import math

import jax
import jax.numpy as jnp
from jax import lax
from jax.experimental import pallas as pl
from jax.experimental.pallas import tpu as pltpu

F32 = jnp.float32
BF16 = jnp.bfloat16
EPS = 1e-6

V7X_VMEM_LIMIT_BYTES = 56 * 1024 * 1024
SUBLANES = 8
LANES = 128
MXU_TILE = 256

M_HEADS = 4
M_CONV = 4
QKV_BLOCK = 4
S_GROUP = 16
S_STATE = 64
H_EXPAND = 128

ROW_TILE_PROJ = 512
ROW_TILE_TAIL = 512
MLSTM_CHUNK = 256
S5_TILE = 256
S5_TILES_PER_PASS = 8
S5_SCAN_UNROLL = 4
HGRN_TILE = 512
HGRN_CHUNK_UNROLL = 4
HGRN_CHUNK = 64


def _dot(a, b):
    return jnp.dot(a, b, preferred_element_type=F32)


def _dot_nt(a, b):
    return lax.dot_general(a, b, (((1,), (1,)), ((), ())), preferred_element_type=F32)


def _dot_tn(a, b):
    return lax.dot_general(a, b, (((0,), (0,)), ((), ())), preferred_element_type=F32)


def _sigmoid(x):
    return 1.0 / (1.0 + jnp.exp(-x))


def _silu(x):
    return x * _sigmoid(x)


def _log_sigmoid(x):
    return jnp.minimum(x, 0.0) - jnp.log(1.0 + jnp.exp(-jnp.abs(x)))


def _gelu_tanh(x):
    return 0.5 * x * (1.0 + jnp.tanh(math.sqrt(2.0 / math.pi) * (x + 0.044715 * (x * x * x))))


def _rms(x, g):
    return x * lax.rsqrt(jnp.mean(x * x, axis=-1, keepdims=True) + EPS) * g


def _prefix_rows(x, op, identity):
    rows, n = x.shape
    row = lax.broadcasted_iota(jnp.int32, (SUBLANES, n), 0)
    outs = []
    carry = None
    for r in range(rows // SUBLANES):
        t = x[r * SUBLANES:(r + 1) * SUBLANES, :]
        for k in (1, 2, 4):
            t = op(t, jnp.where(row >= k, pltpu.roll(t, k, axis=0), identity))
        if carry is not None:
            t = op(t, carry)
        carry = t[SUBLANES - 1:SUBLANES, :]
        outs.append(t)
    return jnp.concatenate(outs, axis=0)


def _cumsum_rows(x):
    return _prefix_rows(x, jnp.add, 0.0)


def _cummax_rows(x):
    return _prefix_rows(x, jnp.maximum, -jnp.inf)


def _params(n_axes):
    return pltpu.CompilerParams(dimension_semantics=("arbitrary",) * n_axes,
                                vmem_limit_bytes=V7X_VMEM_LIMIT_BYTES)


def _const_spec(shape):
    nd = len(shape)
    return pl.BlockSpec(shape, lambda *_: (0,) * nd, pipeline_mode=pl.Buffered(1))


def _rms_matmul_kernel(x_ref, g_ref, w_ref, o_ref):
    h = _rms(x_ref[...], g_ref[...])
    o_ref[...] = _dot(h.astype(BF16), w_ref[...]).astype(o_ref.dtype)


def _rms_matmul(x2d, g, w, tm):
    t, d = x2d.shape
    n = w.shape[1]
    return pl.pallas_call(
        _rms_matmul_kernel,
        grid=(t // tm,),
        in_specs=[pl.BlockSpec((tm, d), lambda i: (i, 0)), _const_spec((1, d)), _const_spec((d, n))],
        out_specs=pl.BlockSpec((tm, n), lambda i: (i, 0)),
        out_shape=jax.ShapeDtypeStruct((t, n), F32),
        compiler_params=_params(1),
        name="rms_proj",
    )(x2d, g.reshape(1, d), w)


def _mlstm_kernel(p_ref, convw_ref, convb_ref, wq_ref, wk_ref, wv_ref, wif_ref, bif_ref, gain_ref, skip_ref,
                  o_ref, xbuf, c_s, m_s, *, heads):
    chunk = o_ref.shape[1]
    width = o_ref.shape[2]
    hd = width // heads
    taps = convw_ref.shape[0]

    @pl.when(pl.program_id(1) == 0)
    def _init():
        xbuf[0:SUBLANES, :] = jnp.zeros((SUBLANES, width), F32)
        c_s[...] = jnp.zeros_like(c_s)
        m_s[...] = jnp.zeros_like(m_s)

    xm = p_ref[0, :, 0:width]
    zm = p_ref[0, :, width:2 * width]

    xbuf[SUBLANES:SUBLANES + chunk, :] = xm
    conv = convb_ref[...] + convw_ref[taps - 1:taps, :] * xm
    for k in range(taps - 1):
        conv = conv + convw_ref[k:k + 1, :] * xbuf[pl.ds(SUBLANES - (taps - 1) + k, chunk), :]
    xbuf[0:SUBLANES, :] = xm[chunk - SUBLANES:chunk, :]
    xc = _silu(conv)

    xc_b = xc.astype(BF16)
    q = _dot(xc_b, wq_ref[...])
    k = _dot(xc_b, wk_ref[...]) * (hd ** -0.5)
    v = _dot(xm.astype(BF16), wv_ref[...])
    q_b, k_b, v_b = q.astype(BF16), k.astype(BF16), v.astype(BF16)
    gates = (_dot(q_b, wif_ref[0:width, :]) + _dot(k_b, wif_ref[width:2 * width, :])
             + _dot(v_b, wif_ref[2 * width:3 * width, :]) + bif_ref[...])

    log_i = gates[:, 0:LANES]
    b = _cumsum_rows(_log_sigmoid(gates[:, LANES:2 * LANES]))
    g = log_i - b
    g_max = _cummax_rows(g)
    m_prev = m_s[0:1, :]
    m_t = jnp.maximum(b + m_prev, b + g_max)
    row_term = b - m_t
    e_neg_m = jnp.exp(-m_t)
    b_last = b[chunk - 1:chunk, :]
    a_max = b_last + g_max[chunk - 1:chunk, :]
    k_scale = jnp.exp(g - g_max[chunk - 1:chunk, :])
    m_new = jnp.maximum(b_last + m_prev, a_max)
    s_old = jnp.exp(b_last + m_prev - m_new)
    s_new = jnp.exp(a_max - m_new)
    m_s[...] = jnp.broadcast_to(m_new, m_s.shape)
    g_rows = g.T

    ti = lax.broadcasted_iota(jnp.int32, (chunk, chunk), 0)
    si = lax.broadcasted_iota(jnp.int32, (chunk, chunk), 1)
    causal = si <= ti
    ones_b = jnp.ones((chunk, hd), BF16)

    for h in range(heads):
        sl = slice(h * hd, (h + 1) * hd)
        kh = k[:, sl]
        qh_b, kh_b = q_b[:, sl], k_b[:, sl]
        v_ones = jnp.concatenate([v_b[:, sl], ones_b], axis=1)
        row_b = jnp.broadcast_to(row_term[:, h:h + 1], (chunk, hd))
        row_w = jnp.concatenate([row_b] * (chunk // hd), axis=1)
        w_intra = jnp.exp(jnp.where(causal, row_w + g_rows[h:h + 1, :], -jnp.inf))
        s_inter = jnp.exp(row_b + m_prev[:, h:h + 1])
        scores = (_dot_nt(qh_b, kh_b) * w_intra).astype(BF16)
        c_prev = c_s[h]
        tot = _dot(scores, v_ones) + jnp.concatenate([s_inter, s_inter], axis=1) * _dot(qh_b, c_prev.astype(BF16))
        den_floor = jnp.broadcast_to(e_neg_m[:, h:h + 1], (chunk, hd))
        hh = tot[:, 0:hd] / jnp.maximum(jnp.abs(tot[:, hd:2 * hd]), den_floor)

        kw = (kh * jnp.broadcast_to(k_scale[:, h:h + 1], (chunk, hd))).astype(BF16)
        c_s[h] = s_old[:, h:h + 1] * c_prev + s_new[:, h:h + 1] * _dot_tn(kw, v_ones)

        mu = jnp.mean(hh, axis=-1, keepdims=True)
        dev = hh - mu
        var = jnp.mean(dev * dev, axis=-1, keepdims=True)
        hn = dev * lax.rsqrt(var + EPS) * gain_ref[:, sl]
        o_ref[0, :, sl] = ((hn + skip_ref[:, sl] * xc[:, sl]) * _silu(zm[:, sl])).astype(o_ref.dtype)


def _mlstm(proj, conv_w, conv_b, wq, wk, wv, w_if, b_if, gain, skip, heads):
    b, l, _ = proj.shape
    width = conv_w.shape[1]
    hd = width // heads
    chunk = MLSTM_CHUNK
    def block_diag(w):
        nb, d, e = w.shape
        tiled = jnp.tile(w.reshape(nb * d, e), (1, nb))
        same_block = (jnp.arange(nb * d) // d)[:, None] == (jnp.arange(nb * e) // e)[None, :]
        return jnp.where(same_block, tiled, 0.0).astype(BF16)

    wif_pad = (jnp.zeros((w_if.shape[0], 2 * LANES), F32).at[:, :heads].set(w_if[:, :heads])
               .at[:, LANES:LANES + heads].set(w_if[:, heads:])).astype(BF16)
    bif_pad = (jnp.zeros((1, 2 * LANES), F32).at[0, :heads].set(b_if[:heads])
               .at[0, LANES:LANES + heads].set(b_if[heads:]))
    kern = functools.partial(_mlstm_kernel, heads=heads)
    return pl.pallas_call(
        kern,
        grid=(b, l // chunk),
        in_specs=[pl.BlockSpec((1, chunk, 2 * width), lambda bi, i: (bi, i, 0)),
                  _const_spec(conv_w.shape), _const_spec((1, width)),
                  _const_spec((width, width)), _const_spec((width, width)), _const_spec((width, width)),
                  _const_spec((3 * width, 2 * LANES)), _const_spec((1, 2 * LANES)),
                  _const_spec((1, width)), _const_spec((1, width))],
        out_specs=pl.BlockSpec((1, chunk, width), lambda bi, i: (bi, i, 0)),
        out_shape=jax.ShapeDtypeStruct((b, l, width), BF16),
        scratch_shapes=[pltpu.VMEM((chunk + SUBLANES, width), F32),
                        pltpu.VMEM((heads, hd, 2 * hd), F32),
                        pltpu.VMEM((SUBLANES, LANES), F32)],
        compiler_params=_params(2),
        name="mlstm",
    )(proj, conv_w, conv_b.reshape(1, width), block_diag(wq), block_diag(wk), block_diag(wv),
      wif_pad, bif_pad, gain.reshape(1, width), skip.reshape(1, width))


def _permute_rows(pm, x):
    hi = x.astype(BF16)
    rest = x - hi.astype(F32)
    mid = rest.astype(BF16)
    lo = (rest - mid.astype(F32)).astype(BF16)
    return (_dot(pm, hi) + _dot(pm, mid)) + _dot(pm, lo)


def _s5_kernel(u_ref, pm_ref, pmt_ref, wb_ref, cst_ref, wc_ref, d_ref, wglu_ref, bglu_ref, o_ref, xs, carry):
    tm = o_ref.shape[1]
    width = o_ref.shape[2]
    seg = tm // SUBLANES
    n_tiles = wb_ref.shape[0]
    kt = wb_ref.shape[1]
    tiles_per_k = n_tiles // (width // kt)
    xw = 2 * LANES

    @pl.when(pl.program_id(1) == 0)
    def _init():
        carry[...] = jnp.zeros_like(carry)

    u = _permute_rows(pm_ref[...], u_ref[0])
    u_b = u.astype(BF16)
    for j in range(n_tiles):
        kk = j // tiles_per_k
        xs[:, j * xw:(j + 1) * xw] = _dot(u_b[:, kk * kt:(kk + 1) * kt], wb_ref[j])

    row = lax.broadcasted_iota(jnp.int32, (SUBLANES, LANES), 0)

    def recur(t, state, tiles, store):
        r8 = pl.multiple_of(t * SUBLANES, SUBLANES)
        new = []
        for idx, j in enumerate(tiles):
            ar, ai = cst_ref[j, 0], cst_ref[j, 1]
            xr, xi = state[2 * idx], state[2 * idx + 1]
            br = xs[pl.ds(r8, SUBLANES), j * xw:j * xw + LANES]
            bi = xs[pl.ds(r8, SUBLANES), j * xw + LANES:(j + 1) * xw]
            xr, xi = (ar * xr - ai * xi) + br, (ar * xi + ai * xr) + bi
            if store:
                xs[pl.ds(r8, SUBLANES), j * xw:j * xw + LANES] = xr
                xs[pl.ds(r8, SUBLANES), j * xw + LANES:(j + 1) * xw] = xi
            new += [xr, xi]
        return tuple(new)

    for j0 in range(0, n_tiles, S5_TILES_PER_PASS):
        tiles = tuple(range(j0, min(j0 + S5_TILES_PER_PASS, n_tiles)))
        zeros = tuple(jnp.zeros((SUBLANES, LANES), F32) for _ in range(2 * len(tiles)))
        ends = lax.fori_loop(0, seg, functools.partial(recur, tiles=tiles, store=False), zeros,
                             unroll=S5_SCAN_UNROLL)

        starts = []
        for idx, j in enumerate(tiles):
            er, ei = ends[2 * idx], ends[2 * idx + 1]
            for lvl, k in enumerate((1, 2, 4)):
                mr, mi = cst_ref[j, 2 + 2 * lvl], cst_ref[j, 3 + 2 * lvl]
                sr, si = pltpu.roll(er, k, axis=0), pltpu.roll(ei, k, axis=0)
                er, ei = er + (mr * sr - mi * si), ei + (mr * si + mi * sr)
            pr, pi = cst_ref[j, 8], cst_ref[j, 9]
            cr, ci = carry[j, 0], carry[j, 1]
            er, ei = er + (pr * cr - pi * ci), ei + (pr * ci + pi * cr)
            carry[j, 0] = jnp.broadcast_to(er[SUBLANES - 1:SUBLANES, :], (SUBLANES, LANES))
            carry[j, 1] = jnp.broadcast_to(ei[SUBLANES - 1:SUBLANES, :], (SUBLANES, LANES))
            starts.append(jnp.where(row == 0, cr, pltpu.roll(er, 1, axis=0)))
            starts.append(jnp.where(row == 0, ci, pltpu.roll(ei, 1, axis=0)))
        lax.fori_loop(0, seg, functools.partial(recur, tiles=tiles, store=True), tuple(starts),
                      unroll=S5_SCAN_UNROLL)

    n_out = wc_ref.shape[0]
    kc = wc_ref.shape[1]
    y = jnp.concatenate([_dot(xs[:, h * kc:(h + 1) * kc].astype(BF16), wc_ref[h]) for h in range(n_out)], axis=1)
    y = _gelu_tanh(y + d_ref[...] * u)
    gate = _sigmoid(_dot(y.astype(BF16), wglu_ref[...]) + bglu_ref[...])
    o_ref[0] = _dot(pmt_ref[...], (y * gate).astype(BF16)).astype(o_ref.dtype)


def _s5_weights(a_re, a_im, log_dt, b_re, b_im, c_re, c_im, seg):
    groups, n_state, p = b_re.shape
    width = groups * p
    gpt = LANES // n_state
    n_tiles = groups // gpt
    dt = jnp.exp(log_dt)[:, None]
    mag = jnp.exp(a_re * dt)
    ab_re = mag * jnp.cos(a_im * dt)
    ab_im = mag * jnp.sin(a_im * dt)
    inv = 1.0 / (a_re * a_re + a_im * a_im)
    g_re = ((ab_re - 1.0) * a_re + ab_im * a_im) * inv
    g_im = (ab_im * a_re - (ab_re - 1.0) * a_im) * inv
    bb_re = g_re[..., None] * b_re - g_im[..., None] * b_im
    bb_im = g_re[..., None] * b_im + g_im[..., None] * b_re
    kt = min(MXU_TILE, width)
    n_k = width // kt
    tiles_per_k = n_tiles // n_k
    xw = 2 * LANES

    bb = jnp.stack([bb_re, bb_im], axis=0)
    r_in = jnp.transpose(bb, (1, 3, 0, 2))
    r_in = jnp.broadcast_to(r_in[:, :, :, None, :], (groups, p, 2, gpt, n_state)).reshape(width, xw)
    row_g = jnp.arange(width) // p
    col_gg = (jnp.arange(xw) % LANES) // n_state
    dense_in = jnp.where((row_g % gpt)[:, None] == col_gg[None, :], r_in, 0.0).reshape(n_k, kt, xw)
    tile_id = jnp.arange(n_tiles)
    row_tile = (row_g // gpt).reshape(n_k, kt)[tile_id // tiles_per_k]
    wb = jnp.where(row_tile[:, :, None] == tile_id[:, None, None], dense_in[tile_id // tiles_per_k], 0.0)

    cc = jnp.stack([c_re, -c_im], axis=0)
    r_out = jnp.transpose(cc, (1, 0, 3, 2)).reshape(n_tiles, gpt, 2, n_state, p)
    r_out = jnp.transpose(r_out, (0, 2, 1, 3, 4)).reshape(n_tiles * xw, p)
    r_out = jnp.tile(r_out, (1, kt // p))
    out_row_g = (jnp.arange(n_tiles * xw) // xw) * gpt + (jnp.arange(n_tiles * xw) % LANES) // n_state
    out_col_g = jnp.arange(kt) // p
    wc = jnp.where((out_row_g % (kt // p))[:, None] == out_col_g[None, :], r_out, 0.0)
    wc = wc.reshape(n_k, tiles_per_k * xw, kt)

    a1 = (ab_re.reshape(n_tiles, LANES), ab_im.reshape(n_tiles, LANES))

    def cmul(x, y):
        return x[0] * y[0] - x[1] * y[1], x[0] * y[1] + x[1] * y[0]

    a_seg = a1
    assert seg & (seg - 1) == 0, "segment length must be a power of two"
    for _ in range(seg.bit_length() - 1):
        a_seg = cmul(a_seg, a_seg)
    pows = [a_seg]
    for _ in range(SUBLANES - 1):
        pows.append(cmul(pows[-1], a_seg))
    row = jnp.arange(SUBLANES)[None, :, None]
    planes = [jnp.broadcast_to(a1[c][:, None, :], (n_tiles, SUBLANES, LANES)) for c in range(2)]
    for k in (1, 2, 4):
        for c in range(2):
            planes.append(jnp.where(row >= k, pows[k - 1][c][:, None, :], 0.0))
    for c in range(2):
        planes.append(jnp.stack([pw[c] for pw in pows], axis=1))
    cst = jnp.stack(planes, axis=1).astype(F32)
    return wb.astype(BF16), cst, wc.astype(BF16)


def _s5(proj, col_block, a_re, a_im, log_dt, b_re, b_im, c_re, c_im, d_skip, w_glu, b_glu):
    b, l, _ = proj.shape
    width = d_skip.shape[0]
    tm = S5_TILE
    seg = tm // SUBLANES
    wb, cst, wc = _s5_weights(a_re, a_im, log_dt, b_re, b_im, c_re, c_im, seg)
    n_tiles = wb.shape[0]
    rp = jnp.arange(tm)
    pm = ((rp % SUBLANES) * seg + rp // SUBLANES)[:, None] == rp[None, :]
    return pl.pallas_call(
        _s5_kernel,
        grid=(b, l // tm),
        in_specs=[pl.BlockSpec((1, tm, width), lambda bi, i: (bi, i, col_block)),
                  _const_spec((tm, tm)), _const_spec((tm, tm)),
                  _const_spec(wb.shape), _const_spec(cst.shape), _const_spec(wc.shape),
                  _const_spec((1, width)), _const_spec((width, width)), _const_spec((1, width))],
        out_specs=pl.BlockSpec((1, tm, width), lambda bi, i: (bi, i, 0)),
        out_shape=jax.ShapeDtypeStruct((b, l, width), BF16),
        scratch_shapes=[pltpu.VMEM((tm, n_tiles * 2 * LANES), F32),
                        pltpu.VMEM((n_tiles, 2, SUBLANES, LANES), F32)],
        compiler_params=_params(2),
        name="s5",
    )(proj, pm.astype(BF16), pm.T.astype(BF16), wb, cst, wc, d_skip.reshape(1, width), w_glu.astype(BF16),
      b_glu.reshape(1, width))


def _hgrn2_kernel(p_ref, lb_ref, gain_ref, o_ref, st, *, heads, chunk):
    tm = o_ref.shape[1]
    width = o_ref.shape[2]
    e = width // heads

    @pl.when(pl.program_id(1) == 0)
    def _init():
        st[...] = jnp.zeros_like(st)

    lb = lb_ref[...]
    ti = lax.broadcasted_iota(jnp.int32, (chunk, chunk), 0)
    si = lax.broadcasted_iota(jnp.int32, (chunk, chunk), 1)
    causal = si <= ti

    def body(c, _):
        r0 = pl.multiple_of(c * chunk, chunk)
        q = _silu(p_ref[0, pl.ds(r0, chunk), 0:width])
        f = p_ref[0, pl.ds(r0, chunk), width:2 * width]
        v_b = p_ref[0, pl.ds(r0, chunk), 2 * width:3 * width].astype(BF16)
        g = p_ref[0, pl.ds(r0, chunk), 3 * width:4 * width]
        fg = lb + (1.0 - lb) * _sigmoid(f)
        k = 1.0 - fg
        b = _cumsum_rows(jnp.log(fg))
        b_last = b[chunk - 1:chunk, :]
        q_inter = (q * jnp.exp(b)).astype(BF16)
        k_state = (k * jnp.exp(b_last - b)).astype(BF16)
        q_intra = (q * jnp.exp(b - b_last)).astype(BF16)
        decay = jnp.exp(b_last)
        gate = gain_ref[...] * _silu(g)
        for h in range(heads):
            sl = slice(h * e, (h + 1) * e)
            s_t = st[h]
            scores = jnp.where(causal, _dot_nt(q_intra[:, sl], k_state[:, sl]), 0.0)
            o = _dot(scores.astype(BF16), v_b[:, sl]) + _dot_nt(q_inter[:, sl], s_t.astype(BF16))
            st[h] = s_t * decay[:, sl] + _dot_tn(v_b[:, sl], k_state[:, sl])
            o = o * lax.rsqrt(jnp.mean(o * o, axis=-1, keepdims=True) + EPS)
            o_ref[0, pl.ds(r0, chunk), sl] = (o * gate[:, sl]).astype(o_ref.dtype)
        return 0

    lax.fori_loop(0, tm // chunk, body, 0, unroll=HGRN_CHUNK_UNROLL)


def _hgrn2(proj, lb, gain):
    b, l, w4 = proj.shape
    width = w4 // 4
    heads = width // H_EXPAND
    tm = HGRN_TILE
    kern = functools.partial(_hgrn2_kernel, heads=heads, chunk=HGRN_CHUNK)
    return pl.pallas_call(
        kern,
        grid=(b, l // tm),
        in_specs=[pl.BlockSpec((1, tm, w4), lambda bi, i: (bi, i, 0)),
                  _const_spec((1, width)), _const_spec((1, width))],
        out_specs=pl.BlockSpec((1, tm, width), lambda bi, i: (bi, i, 0)),
        out_shape=jax.ShapeDtypeStruct((b, l, width), BF16),
        scratch_shapes=[pltpu.VMEM((heads, H_EXPAND, H_EXPAND), F32)],
        compiler_params=_params(2),
        name="hgrn2",
    )(proj, lb.reshape(1, width), gain.reshape(1, width))


def _tail_kernel(*refs, n_mix):
    x_ref = refs[0]
    mix_refs = refs[1:1 + n_mix]
    wo_ref, g1_ref, g2_ref, g3_ref, w1_ref, w3_ref, w2_ref, o_ref = refs[1 + n_mix:]
    mp = None
    row0 = 0
    for m_ref in mix_refs:
        part = _dot(m_ref[...], wo_ref[row0:row0 + m_ref.shape[1], :])
        mp = part if mp is None else mp + part
        row0 += m_ref.shape[1]
    x1 = x_ref[...] + _rms(mp, g1_ref[...])
    h = _rms(x1, g2_ref[...]).astype(BF16)
    a = _dot(h, w1_ref[...])
    z = (_silu(a) * _dot(h, w3_ref[...])).astype(BF16)
    y = _dot(z, w2_ref[...])
    o_ref[...] = x1 + _rms(y, g3_ref[...])


def _block_tail(x2d, mixes, w_out, g1, g2, g3, w1, w3, w2):
    t, d = x2d.shape
    f = w1.shape[1]
    tm = ROW_TILE_TAIL
    n_mix = len(mixes)
    row = lambda i: (i, 0)
    in_specs = ([pl.BlockSpec((tm, d), row)]
                + [pl.BlockSpec((tm, m.shape[1]), row) for m in mixes]
                + [_const_spec(w_out.shape)]
                + [_const_spec((1, d))] * 3
                + [_const_spec((d, f)), _const_spec((d, f)), _const_spec((f, d))])
    return pl.pallas_call(
        functools.partial(_tail_kernel, n_mix=n_mix),
        grid=(t // tm,),
        in_specs=in_specs,
        out_specs=pl.BlockSpec((tm, d), row),
        out_shape=jax.ShapeDtypeStruct((t, d), F32),
        compiler_params=_params(1),
        name="outproj_swiglu",
    )(x2d, *mixes, w_out.astype(BF16), g1.reshape(1, d), g2.reshape(1, d), g3.reshape(1, d),
      w1.astype(BF16), w3.astype(BF16), w2.astype(BF16))


def kernel(x, norm_g, ab_w_in, ab_conv_w, ab_conv_b, ab_wq, ab_wk, ab_wv, ab_w_if, ab_b_if, ab_mh_gain, ab_skip,
           ab_a_re, ab_a_im, ab_log_dt, ab_b_re, ab_b_im, ab_c_re, ab_c_im, ab_d, ab_w_glu, ab_b_glu, ab_w_out,
           c_w_in, c_lb_raw, c_g_gain, c_w_out, ffn_w1, ffn_w3, ffn_w2):
    bsz, seq, d = x.shape
    t = bsz * seq
    depth = norm_g.shape[0]
    x2d = x.reshape(t, d)
    lbs = jnp.cumsum(jax.nn.softmax(c_lb_raw.astype(F32), axis=0), axis=0)
    for layer in range(depth):
        j = layer // 2
        if layer % 2 == 0:
            m_width = ab_conv_w.shape[2]
            s_width = ab_d.shape[1]
            proj = _rms_matmul(x2d, norm_g[layer, 0], ab_w_in[j].astype(BF16), ROW_TILE_PROJ).reshape(bsz, seq, -1)
            out_m = _mlstm(proj, ab_conv_w[j], ab_conv_b[j], ab_wq[j], ab_wk[j], ab_wv[j], ab_w_if[j], ab_b_if[j],
                           ab_mh_gain[j], ab_skip[j], M_HEADS)
            out_s = _s5(proj, (2 * m_width) // s_width, ab_a_re[j], ab_a_im[j], ab_log_dt[j], ab_b_re[j], ab_b_im[j],
                        ab_c_re[j], ab_c_im[j], ab_d[j], ab_w_glu[j], ab_b_glu[j])
            mixes = [out_m.reshape(t, m_width), out_s.reshape(t, s_width)]
            w_out = ab_w_out[j]
        else:
            proj = _rms_matmul(x2d, norm_g[layer, 0], c_w_in[j].astype(BF16), ROW_TILE_PROJ).reshape(bsz, seq, -1)
            lb = lbs[layer] - lbs[0]
            mixes = [_hgrn2(proj, lb, c_g_gain[j]).reshape(t, -1)]
            w_out = c_w_out[j]
        x2d = _block_tail(x2d, mixes, w_out, norm_g[layer, 1], norm_g[layer, 2], norm_g[layer, 3],
                          ffn_w1[layer], ffn_w3[layer], ffn_w2[layer])
    return x2d.reshape(bsz, seq, d)
```

```python
import functools
import math

import jax
import jax.numpy as jnp
from jax import lax
from jax.experimental import pallas as pl
from jax.experimental.pallas import tpu as pltpu

F32 = jnp.float32
BF16 = jnp.bfloat16
EPS = 1e-6

V7X_VMEM_LIMIT_BYTES = 56 * 1024 * 1024
SUBLANES = 8
LANES = 128
MXU_TILE = 256

M_HEADS = 4
M_CONV = 4
QKV_BLOCK = 4
S_GROUP = 16
S_STATE = 64
H_EXPAND = 128

ROW_TILE_PROJ = 512
ROW_TILE_TAIL = 512
MLSTM_CHUNK = 256
S5_TILE = 256
S5_TILES_PER_PASS = 8
S5_SCAN_UNROLL = 4
HGRN_TILE = 512
HGRN_CHUNK_UNROLL = 4
TAIL_ROW_GROUPS = 2
HGRN_CHUNK = 64


def _dot(a, b):
    return jnp.dot(a, b, preferred_element_type=F32)


def _dot_nt(a, b):
    return lax.dot_general(a, b, (((1,), (1,)), ((), ())), preferred_element_type=F32)


def _dot_tn(a, b):
    return lax.dot_general(a, b, (((0,), (0,)), ((), ())), preferred_element_type=F32)


def _sigmoid(x):
    return 1.0 / (1.0 + jnp.exp(-x))


def _silu(x):
    return x * _sigmoid(x)


def _log_sigmoid(x):
    return jnp.minimum(x, 0.0) - jnp.log(1.0 + jnp.exp(-jnp.abs(x)))


def _gelu_tanh(x):
    return 0.5 * x * (1.0 + jnp.tanh(math.sqrt(2.0 / math.pi) * (x + 0.044715 * (x * x * x))))


def _rms(x, g):
    return x * lax.rsqrt(jnp.mean(x * x, axis=-1, keepdims=True) + EPS) * g


def _prefix_rows(x, op, identity):
    rows, n = x.shape
    row = lax.broadcasted_iota(jnp.int32, (SUBLANES, n), 0)
    outs = []
    carry = None
    for r in range(rows // SUBLANES):
        t = x[r * SUBLANES:(r + 1) * SUBLANES, :]
        for k in (1, 2, 4):
            t = op(t, jnp.where(row >= k, pltpu.roll(t, k, axis=0), identity))
        if carry is not None:
            t = op(t, carry)
        carry = t[SUBLANES - 1:SUBLANES, :]
        outs.append(t)
    return jnp.concatenate(outs, axis=0)


def _cumsum_rows(x):
    return _prefix_rows(x, jnp.add, 0.0)


def _cummax_rows(x):
    return _prefix_rows(x, jnp.maximum, -jnp.inf)


def _params(n_axes):
    return pltpu.CompilerParams(dimension_semantics=("arbitrary",) * n_axes,
                                vmem_limit_bytes=V7X_VMEM_LIMIT_BYTES)


def _const_spec(shape):
    nd = len(shape)
    return pl.BlockSpec(shape, lambda *_: (0,) * nd, pipeline_mode=pl.Buffered(1))


def _rms_matmul_kernel(x_ref, g_ref, w_ref, o_ref):
    h = _rms(x_ref[...], g_ref[...])
    o_ref[...] = _dot(h.astype(BF16), w_ref[...]).astype(o_ref.dtype)


def _rms_matmul(x2d, g, w, tm):
    t, d = x2d.shape
    n = w.shape[1]
    return pl.pallas_call(
        _rms_matmul_kernel,
        grid=(t // tm,),
        in_specs=[pl.BlockSpec((tm, d), lambda i: (i, 0)), _const_spec((1, d)), _const_spec((d, n))],
        out_specs=pl.BlockSpec((tm, n), lambda i: (i, 0)),
        out_shape=jax.ShapeDtypeStruct((t, n), F32),
        compiler_params=_params(1),
        name="rms_proj",
    )(x2d, g.reshape(1, d), w)


def _mlstm_kernel(p_ref, convw_ref, convb_ref, wq_ref, wk_ref, wv_ref, wif_ref, bif_ref, gain_ref, skip_ref,
                  o_ref, xbuf, c_s, m_s, *, heads):
    n_rows, chunk, width = o_ref.shape
    hd = width // heads
    taps = convw_ref.shape[0]
    rows = range(n_rows)

    @pl.when(pl.program_id(0) == 0)
    def _init():
        xbuf[:, 0:SUBLANES, :] = jnp.zeros((n_rows, SUBLANES, width), F32)
        c_s[...] = jnp.zeros_like(c_s)
        m_s[...] = jnp.zeros_like(m_s)

    xm = [p_ref[r, :, 0:width] for r in rows]
    zm = [p_ref[r, :, width:2 * width] for r in rows]

    xc = []
    for r in rows:
        xbuf[r, SUBLANES:SUBLANES + chunk, :] = xm[r]
        conv = convb_ref[...] + convw_ref[taps - 1:taps, :] * xm[r]
        for k in range(taps - 1):
            conv = conv + convw_ref[k:k + 1, :] * xbuf[r, pl.ds(SUBLANES - (taps - 1) + k, chunk), :]
        xbuf[r, 0:SUBLANES, :] = xm[r][chunk - SUBLANES:chunk, :]
        xc.append(_silu(conv))

    q_b, k, k_b, v_b, gates = [], [], [], [], []
    for r in rows:
        xc_b = xc[r].astype(BF16)
        q_b.append(_dot(xc_b, wq_ref[...]).astype(BF16))
        k.append(_dot(xc_b, wk_ref[...]) * (hd ** -0.5))
        k_b.append(k[r].astype(BF16))
        v_b.append(_dot(xm[r].astype(BF16), wv_ref[...]).astype(BF16))
        gates.append(_dot(q_b[r], wif_ref[0:width, :]) + _dot(k_b[r], wif_ref[width:2 * width, :])
                     + _dot(v_b[r], wif_ref[2 * width:3 * width, :]) + bif_ref[...])

    m_prev, row_term, e_neg_m, k_scale, s_old, s_new, g_rows = [], [], [], [], [], [], []
    for r in rows:
        log_i = gates[r][:, 0:LANES]
        b = _cumsum_rows(_log_sigmoid(gates[r][:, LANES:2 * LANES]))
        g = log_i - b
        g_max = _cummax_rows(g)
        m_prev.append(m_s[r, 0:1, :])
        m_t = jnp.maximum(b + m_prev[r], b + g_max)
        row_term.append(b - m_t)
        e_neg_m.append(jnp.exp(-m_t))
        b_last = b[chunk - 1:chunk, :]
        a_max = b_last + g_max[chunk - 1:chunk, :]
        k_scale.append(jnp.exp(g - g_max[chunk - 1:chunk, :]))
        m_new = jnp.maximum(b_last + m_prev[r], a_max)
        s_old.append(jnp.exp(b_last + m_prev[r] - m_new))
        s_new.append(jnp.exp(a_max - m_new))
        m_s[r] = jnp.broadcast_to(m_new, m_s.shape[1:])
        g_rows.append(g.T)

    causal = _causal_mask(chunk)
    ones_b = jnp.ones((chunk, hd), BF16)

    for h in range(heads):
        sl = slice(h * hd, (h + 1) * hd)
        for r in rows:
            qh_b, kh_b = q_b[r][:, sl], k_b[r][:, sl]
            v_ones = jnp.concatenate([v_b[r][:, sl], ones_b], axis=1)
            row_b = jnp.broadcast_to(row_term[r][:, h:h + 1], (chunk, hd))
            row_w = jnp.concatenate([row_b] * (chunk // hd), axis=1)
            w_intra = jnp.exp(jnp.where(causal, row_w + g_rows[r][h:h + 1, :], -jnp.inf))
            s_inter = jnp.exp(row_b + m_prev[r][:, h:h + 1])
            scores = (_dot_nt(qh_b, kh_b) * w_intra).astype(BF16)
            c_prev = c_s[r, h]
            tot = (_dot(scores, v_ones)
                   + jnp.concatenate([s_inter, s_inter], axis=1) * _dot(qh_b, c_prev.astype(BF16)))
            den_floor = jnp.broadcast_to(e_neg_m[r][:, h:h + 1], (chunk, hd))
            hh = tot[:, 0:hd] / jnp.maximum(jnp.abs(tot[:, hd:2 * hd]), den_floor)

            kw = (k[r][:, sl] * jnp.broadcast_to(k_scale[r][:, h:h + 1], (chunk, hd))).astype(BF16)
            c_s[r, h] = s_old[r][:, h:h + 1] * c_prev + s_new[r][:, h:h + 1] * _dot_tn(kw, v_ones)

            mu = jnp.mean(hh, axis=-1, keepdims=True)
            dev = hh - mu
            var = jnp.mean(dev * dev, axis=-1, keepdims=True)
            hn = dev * lax.rsqrt(var + EPS) * gain_ref[:, sl]
            o_ref[r, :, sl] = ((hn + skip_ref[:, sl] * xc[r][:, sl]) * _silu(zm[r][:, sl])).astype(o_ref.dtype)


def _mlstm(proj, conv_w, conv_b, wq, wk, wv, w_if, b_if, gain, skip, heads):
    b, l, _ = proj.shape
    width = conv_w.shape[1]
    hd = width // heads
    chunk = MLSTM_CHUNK
    def block_diag(w):
        nb, d, e = w.shape
        tiled = jnp.tile(w.reshape(nb * d, e), (1, nb))
        same_block = (jnp.arange(nb * d) // d)[:, None] == (jnp.arange(nb * e) // e)[None, :]
        return jnp.where(same_block, tiled, 0.0).astype(BF16)

    wif_pad = (jnp.zeros((w_if.shape[0], 2 * LANES), F32).at[:, :heads].set(w_if[:, :heads])
               .at[:, LANES:LANES + heads].set(w_if[:, heads:])).astype(BF16)
    bif_pad = (jnp.zeros((1, 2 * LANES), F32).at[0, :heads].set(b_if[:heads])
               .at[0, LANES:LANES + heads].set(b_if[heads:]))
    kern = functools.partial(_mlstm_kernel, heads=heads)
    return pl.pallas_call(
        kern,
        grid=(l // chunk,),
        in_specs=[pl.BlockSpec((b, chunk, 2 * width), lambda i: (0, i, 0)),
                  _const_spec(conv_w.shape), _const_spec((1, width)),
                  _const_spec((width, width)), _const_spec((width, width)), _const_spec((width, width)),
                  _const_spec((3 * width, 2 * LANES)), _const_spec((1, 2 * LANES)),
                  _const_spec((1, width)), _const_spec((1, width))],
        out_specs=pl.BlockSpec((b, chunk, width), lambda i: (0, i, 0)),
        out_shape=jax.ShapeDtypeStruct((b, l, width), BF16),
        scratch_shapes=[pltpu.VMEM((b, chunk + SUBLANES, width), F32),
                        pltpu.VMEM((b, heads, hd, 2 * hd), F32),
                        pltpu.VMEM((b, SUBLANES, LANES), F32)],
        compiler_params=_params(1),
        name="mlstm",
    )(proj, conv_w, conv_b.reshape(1, width), block_diag(wq), block_diag(wk), block_diag(wv),
      wif_pad, bif_pad, gain.reshape(1, width), skip.reshape(1, width))


def _permute_rows(pm, x):
    hi = x.astype(BF16)
    rest = x - hi.astype(F32)
    mid = rest.astype(BF16)
    lo = (rest - mid.astype(F32)).astype(BF16)
    return (_dot(pm, hi) + _dot(pm, mid)) + _dot(pm, lo)


def _s5_kernel(u_ref, pm_ref, pmt_ref, wb_ref, cst_ref, wc_ref, d_ref, wglu_ref, bglu_ref, o_ref, xs, carry):
    tm = o_ref.shape[1]
    width = o_ref.shape[2]
    seg = tm // SUBLANES
    n_tiles = wb_ref.shape[0]
    kt = wb_ref.shape[1]
    tiles_per_k = n_tiles // (width // kt)
    xw = 2 * LANES

    @pl.when(pl.program_id(1) == 0)
    def _init():
        carry[...] = jnp.zeros_like(carry)

    u = _permute_rows(pm_ref[...], u_ref[0])
    u_b = u.astype(BF16)
    for j in range(n_tiles):
        kk = j // tiles_per_k
        xs[:, j * xw:(j + 1) * xw] = _dot(u_b[:, kk * kt:(kk + 1) * kt], wb_ref[j])

    row = lax.broadcasted_iota(jnp.int32, (SUBLANES, LANES), 0)

    def recur(t, state, tiles, store):
        r8 = pl.multiple_of(t * SUBLANES, SUBLANES)
        new = []
        for idx, j in enumerate(tiles):
            ar, ai = cst_ref[j, 0], cst_ref[j, 1]
            xr, xi = state[2 * idx], state[2 * idx + 1]
            br = xs[pl.ds(r8, SUBLANES), j * xw:j * xw + LANES]
            bi = xs[pl.ds(r8, SUBLANES), j * xw + LANES:(j + 1) * xw]
            xr, xi = (ar * xr - ai * xi) + br, (ar * xi + ai * xr) + bi
            if store:
                xs[pl.ds(r8, SUBLANES), j * xw:j * xw + LANES] = xr
                xs[pl.ds(r8, SUBLANES), j * xw + LANES:(j + 1) * xw] = xi
            new += [xr, xi]
        return tuple(new)

    for j0 in range(0, n_tiles, S5_TILES_PER_PASS):
        tiles = tuple(range(j0, min(j0 + S5_TILES_PER_PASS, n_tiles)))
        zeros = tuple(jnp.zeros((SUBLANES, LANES), F32) for _ in range(2 * len(tiles)))
        ends = lax.fori_loop(0, seg, functools.partial(recur, tiles=tiles, store=False), zeros,
                             unroll=S5_SCAN_UNROLL)

        starts = []
        for idx, j in enumerate(tiles):
            er, ei = ends[2 * idx], ends[2 * idx + 1]
            for lvl, k in enumerate((1, 2, 4)):
                mr, mi = cst_ref[j, 2 + 2 * lvl], cst_ref[j, 3 + 2 * lvl]
                sr, si = pltpu.roll(er, k, axis=0), pltpu.roll(ei, k, axis=0)
                er, ei = er + (mr * sr - mi * si), ei + (mr * si + mi * sr)
            pr, pi = cst_ref[j, 8], cst_ref[j, 9]
            cr, ci = carry[j, 0], carry[j, 1]
            er, ei = er + (pr * cr - pi * ci), ei + (pr * ci + pi * cr)
            carry[j, 0] = jnp.broadcast_to(er[SUBLANES - 1:SUBLANES, :], (SUBLANES, LANES))
            carry[j, 1] = jnp.broadcast_to(ei[SUBLANES - 1:SUBLANES, :], (SUBLANES, LANES))
            starts.append(jnp.where(row == 0, cr, pltpu.roll(er, 1, axis=0)))
            starts.append(jnp.where(row == 0, ci, pltpu.roll(ei, 1, axis=0)))
        lax.fori_loop(0, seg, functools.partial(recur, tiles=tiles, store=True), tuple(starts),
                      unroll=S5_SCAN_UNROLL)

    n_out = wc_ref.shape[0]
    kc = wc_ref.shape[1]
    y = jnp.concatenate([_dot(xs[:, h * kc:(h + 1) * kc].astype(BF16), wc_ref[h]) for h in range(n_out)], axis=1)
    y = _gelu_tanh(y + d_ref[...] * u)
    gate = _sigmoid(_dot(y.astype(BF16), wglu_ref[...]) + bglu_ref[...])
    o_ref[0] = _dot(pmt_ref[...], (y * gate).astype(BF16)).astype(o_ref.dtype)


def _s5_weights(a_re, a_im, log_dt, b_re, b_im, c_re, c_im, seg):
    groups, n_state, p = b_re.shape
    width = groups * p
    gpt = LANES // n_state
    n_tiles = groups // gpt
    dt = jnp.exp(log_dt)[:, None]
    mag = jnp.exp(a_re * dt)
    ab_re = mag * jnp.cos(a_im * dt)
    ab_im = mag * jnp.sin(a_im * dt)
    inv = 1.0 / (a_re * a_re + a_im * a_im)
    g_re = ((ab_re - 1.0) * a_re + ab_im * a_im) * inv
    g_im = (ab_im * a_re - (ab_re - 1.0) * a_im) * inv
    bb_re = g_re[..., None] * b_re - g_im[..., None] * b_im
    bb_im = g_re[..., None] * b_im + g_im[..., None] * b_re
    kt = min(MXU_TILE, width)
    n_k = width // kt
    tiles_per_k = n_tiles // n_k
    xw = 2 * LANES

    bb = jnp.stack([bb_re, bb_im], axis=0)
    r_in = jnp.transpose(bb, (1, 3, 0, 2))
    r_in = jnp.broadcast_to(r_in[:, :, :, None, :], (groups, p, 2, gpt, n_state)).reshape(width, xw)
    row_g = jnp.arange(width) // p
    col_gg = (jnp.arange(xw) % LANES) // n_state
    dense_in = jnp.where((row_g % gpt)[:, None] == col_gg[None, :], r_in, 0.0).reshape(n_k, kt, xw)
    tile_id = jnp.arange(n_tiles)
    row_tile = (row_g // gpt).reshape(n_k, kt)[tile_id // tiles_per_k]
    wb = jnp.where(row_tile[:, :, None] == tile_id[:, None, None], dense_in[tile_id // tiles_per_k], 0.0)

    cc = jnp.stack([c_re, -c_im], axis=0)
    r_out = jnp.transpose(cc, (1, 0, 3, 2)).reshape(n_tiles, gpt, 2, n_state, p)
    r_out = jnp.transpose(r_out, (0, 2, 1, 3, 4)).reshape(n_tiles * xw, p)
    r_out = jnp.tile(r_out, (1, kt // p))
    out_row_g = (jnp.arange(n_tiles * xw) // xw) * gpt + (jnp.arange(n_tiles * xw) % LANES) // n_state
    out_col_g = jnp.arange(kt) // p
    wc = jnp.where((out_row_g % (kt // p))[:, None] == out_col_g[None, :], r_out, 0.0)
    wc = wc.reshape(n_k, tiles_per_k * xw, kt)

    a1 = (ab_re.reshape(n_tiles, LANES), ab_im.reshape(n_tiles, LANES))

    def cmul(x, y):
        return x[0] * y[0] - x[1] * y[1], x[0] * y[1] + x[1] * y[0]

    a_seg = a1
    assert seg & (seg - 1) == 0, "segment length must be a power of two"
    for _ in range(seg.bit_length() - 1):
        a_seg = cmul(a_seg, a_seg)
    pows = [a_seg]
    for _ in range(SUBLANES - 1):
        pows.append(cmul(pows[-1], a_seg))
    row = jnp.arange(SUBLANES)[None, :, None]
    planes = [jnp.broadcast_to(a1[c][:, None, :], (n_tiles, SUBLANES, LANES)) for c in range(2)]
    for k in (1, 2, 4):
        for c in range(2):
            planes.append(jnp.where(row >= k, pows[k - 1][c][:, None, :], 0.0))
    for c in range(2):
        planes.append(jnp.stack([pw[c] for pw in pows], axis=1))
    cst = jnp.stack(planes, axis=1).astype(F32)
    return wb.astype(BF16), cst, wc.astype(BF16)


def _s5(proj, col_block, a_re, a_im, log_dt, b_re, b_im, c_re, c_im, d_skip, w_glu, b_glu):
    b, l, _ = proj.shape
    width = d_skip.shape[0]
    tm = S5_TILE
    seg = tm // SUBLANES
    wb, cst, wc = _s5_weights(a_re, a_im, log_dt, b_re, b_im, c_re, c_im, seg)
    n_tiles = wb.shape[0]
    rp = jnp.arange(tm)
    pm = ((rp % SUBLANES) * seg + rp // SUBLANES)[:, None] == rp[None, :]
    return pl.pallas_call(
        _s5_kernel,
        grid=(b, l // tm),
        in_specs=[pl.BlockSpec((1, tm, width), lambda bi, i: (bi, i, col_block)),
                  _const_spec((tm, tm)), _const_spec((tm, tm)),
                  _const_spec(wb.shape), _const_spec(cst.shape), _const_spec(wc.shape),
                  _const_spec((1, width)), _const_spec((width, width)), _const_spec((1, width))],
        out_specs=pl.BlockSpec((1, tm, width), lambda bi, i: (bi, i, 0)),
        out_shape=jax.ShapeDtypeStruct((b, l, width), BF16),
        scratch_shapes=[pltpu.VMEM((tm, n_tiles * 2 * LANES), F32),
                        pltpu.VMEM((n_tiles, 2, SUBLANES, LANES), F32)],
        compiler_params=_params(2),
        name="s5",
    )(proj, pm.astype(BF16), pm.T.astype(BF16), wb, cst, wc, d_skip.reshape(1, width), w_glu.astype(BF16),
      b_glu.reshape(1, width))


def _hgrn2_chunk(p_ref, r0, lb, gain, st, causal, o_ref, *, heads, chunk):
    width = o_ref.shape[1]
    e = width // heads
    rows = pl.ds(r0, chunk)
    for h in range(heads):
        sl = slice(h * e, (h + 1) * e)
        q = _silu(p_ref[rows, h * e:(h + 1) * e])
        f = p_ref[rows, width + h * e:width + (h + 1) * e]
        v_b = p_ref[rows, 2 * width + h * e:2 * width + (h + 1) * e].astype(BF16)
        g = p_ref[rows, 3 * width + h * e:3 * width + (h + 1) * e]
        lb_h = lb[:, sl]
        fg = lb_h + (1.0 - lb_h) * _sigmoid(f)
        b = _cumsum_rows(jnp.log(fg))
        b_last = b[chunk - 1:chunk, :]
        q_inter = (q * jnp.exp(b)).astype(BF16)
        k_state = ((1.0 - fg) * jnp.exp(b_last - b)).astype(BF16)
        q_intra = (q * jnp.exp(b - b_last)).astype(BF16)
        s_t = st[h]
        scores = jnp.where(causal, _dot_nt(q_intra, k_state), 0.0)
        o = _dot(scores.astype(BF16), v_b) + _dot_nt(q_inter, s_t.astype(BF16))
        st[h] = s_t * jnp.exp(b_last) + _dot_tn(v_b, k_state)
        o = o * lax.rsqrt(jnp.mean(o * o, axis=-1, keepdims=True) + EPS)
        o_ref[rows, sl] = (o * (gain[:, sl] * _silu(g))).astype(o_ref.dtype)


def _causal_mask(chunk):
    ti = lax.broadcasted_iota(jnp.int32, (chunk, chunk), 0)
    si = lax.broadcasted_iota(jnp.int32, (chunk, chunk), 1)
    return si <= ti


def _tail_math(xs, mix_groups, wo_ref, g1_ref, g2_ref, g3_ref, w1_ref, w3_ref, w2_ref):
    mps = []
    for mix_refs in mix_groups:
        mp = None
        row0 = 0
        for m_ref in mix_refs:
            part = _dot(m_ref[...], wo_ref[row0:row0 + m_ref.shape[1], :])
            mp = part if mp is None else mp + part
            row0 += m_ref.shape[1]
        mps.append(mp)
    x1s = [x + _rms(mp, g1_ref[...]) for x, mp in zip(xs, mps)]
    hs = [_rms(x1, g2_ref[...]).astype(BF16) for x1 in x1s]
    ys = []
    for h in hs:
        z = (_silu(_dot(h, w1_ref[...])) * _dot(h, w3_ref[...])).astype(BF16)
        ys.append(_dot(z, w2_ref[...]))
    return [x1 + _rms(y, g3_ref[...]) for x1, y in zip(x1s, ys)]


def _hgrn2_kernel(p_ref, lb_ref, gain_ref, o_ref, st, *, heads, chunk):
    tm = o_ref.shape[1]

    @pl.when(pl.program_id(1) == 0)
    def _init():
        st[...] = jnp.zeros_like(st)

    lb = lb_ref[...]
    gain = gain_ref[...]
    causal = _causal_mask(chunk)

    def body(c, _):
        _hgrn2_chunk(p_ref.at[0], pl.multiple_of(c * chunk, chunk), lb, gain, st, causal, o_ref.at[0],
                     heads=heads, chunk=chunk)
        return 0

    lax.fori_loop(0, tm // chunk, body, 0, unroll=HGRN_CHUNK_UNROLL)


def _hgrn2(proj, lb, gain):
    b, l, w4 = proj.shape
    width = w4 // 4
    heads = width // H_EXPAND
    tm = HGRN_TILE
    kern = functools.partial(_hgrn2_kernel, heads=heads, chunk=HGRN_CHUNK)
    return pl.pallas_call(
        kern,
        grid=(b, l // tm),
        in_specs=[pl.BlockSpec((1, tm, w4), lambda bi, i: (bi, i, 0)),
                  _const_spec((1, width)), _const_spec((1, width))],
        out_specs=pl.BlockSpec((1, tm, width), lambda bi, i: (bi, i, 0)),
        out_shape=jax.ShapeDtypeStruct((b, l, width), BF16),
        scratch_shapes=[pltpu.VMEM((heads, H_EXPAND, H_EXPAND), F32)],
        compiler_params=_params(2),
        name="hgrn2",
    )(proj, lb.reshape(1, width), gain.reshape(1, width))


def _tail_kernel(*refs, n_mix):
    x_ref = refs[0]
    mix_refs = refs[1:1 + n_mix]
    wo_ref, g1_ref, g2_ref, g3_ref, w1_ref, w3_ref, w2_ref, o_ref = refs[1 + n_mix:]
    rows = o_ref.shape[0] // TAIL_ROW_GROUPS
    groups = [slice(r * rows, (r + 1) * rows) for r in range(TAIL_ROW_GROUPS)]
    outs = _tail_math([x_ref[sl, :] for sl in groups], [[m.at[sl, :] for m in mix_refs] for sl in groups],
                      wo_ref, g1_ref, g2_ref, g3_ref, w1_ref, w3_ref, w2_ref)
    for sl, out in zip(groups, outs):
        o_ref[sl, :] = out


def _block_tail(x2d, mixes, w_out, g1, g2, g3, w1, w3, w2):
    t, d = x2d.shape
    f = w1.shape[1]
    tm = ROW_TILE_TAIL
    n_mix = len(mixes)
    row = lambda i: (i, 0)
    in_specs = ([pl.BlockSpec((tm, d), row)]
                + [pl.BlockSpec((tm, m.shape[1]), row) for m in mixes]
                + [_const_spec(w_out.shape)]
                + [_const_spec((1, d))] * 3
                + [_const_spec((d, f)), _const_spec((d, f)), _const_spec((f, d))])
    return pl.pallas_call(
        functools.partial(_tail_kernel, n_mix=n_mix),
        grid=(t // tm,),
        in_specs=in_specs,
        out_specs=pl.BlockSpec((tm, d), row),
        out_shape=jax.ShapeDtypeStruct((t, d), F32),
        compiler_params=_params(1),
        name="outproj_swiglu",
    )(x2d, *mixes, w_out.astype(BF16), g1.reshape(1, d), g2.reshape(1, d), g3.reshape(1, d),
      w1.astype(BF16), w3.astype(BF16), w2.astype(BF16))


def kernel(x, norm_g, ab_w_in, ab_conv_w, ab_conv_b, ab_wq, ab_wk, ab_wv, ab_w_if, ab_b_if, ab_mh_gain, ab_skip,
           ab_a_re, ab_a_im, ab_log_dt, ab_b_re, ab_b_im, ab_c_re, ab_c_im, ab_d, ab_w_glu, ab_b_glu, ab_w_out,
           c_w_in, c_lb_raw, c_g_gain, c_w_out, ffn_w1, ffn_w3, ffn_w2):
    bsz, seq, d = x.shape
    t = bsz * seq
    depth = norm_g.shape[0]
    x2d = x.reshape(t, d)
    lbs = jnp.cumsum(jax.nn.softmax(c_lb_raw.astype(F32), axis=0), axis=0)
    for layer in range(depth):
        j = layer // 2
        if layer % 2 == 0:
            m_width = ab_conv_w.shape[2]
            s_width = ab_d.shape[1]
            proj = _rms_matmul(x2d, norm_g[layer, 0], ab_w_in[j].astype(BF16), ROW_TILE_PROJ).reshape(bsz, seq, -1)
            out_m = _mlstm(proj, ab_conv_w[j], ab_conv_b[j], ab_wq[j], ab_wk[j], ab_wv[j], ab_w_if[j], ab_b_if[j],
                           ab_mh_gain[j], ab_skip[j], M_HEADS)
            out_s = _s5(proj, (2 * m_width) // s_width, ab_a_re[j], ab_a_im[j], ab_log_dt[j], ab_b_re[j], ab_b_im[j],
                        ab_c_re[j], ab_c_im[j], ab_d[j], ab_w_glu[j], ab_b_glu[j])
            mixes = [out_m.reshape(t, m_width), out_s.reshape(t, s_width)]
            w_out = ab_w_out[j]
        else:
            proj = _rms_matmul(x2d, norm_g[layer, 0], c_w_in[j].astype(BF16), ROW_TILE_PROJ).reshape(bsz, seq, -1)
            lb = lbs[layer] - lbs[0]
            mixes = [_hgrn2(proj, lb, c_g_gain[j]).reshape(t, -1)]
            w_out = c_w_out[j]
        x2d = _block_tail(x2d, mixes, w_out, norm_g[layer, 1], norm_g[layer, 2], norm_g[layer, 3],
                          ffn_w1[layer], ffn_w3[layer], ffn_w2[layer])
    return x2d.reshape(bsz, seq, d)
```

```python
import functools
import math

import jax
import jax.numpy as jnp
from jax import lax
from jax.experimental import pallas as pl
from jax.experimental.pallas import tpu as pltpu

F32 = jnp.float32
BF16 = jnp.bfloat16
EPS = 1e-6

V7X_VMEM_LIMIT_BYTES = 56 * 1024 * 1024
SUBLANES = 8
LANES = 128
MXU_TILE = 256

M_HEADS = 4
M_CONV = 4
QKV_BLOCK = 4
S_GROUP = 16
S_STATE = 64
H_EXPAND = 128

ROW_TILE_PROJ = 512
ROW_TILE_TAIL = 512
MLSTM_CHUNK = 256
S5_TILE = 256
S5_TILES_PER_PASS = 8
HGRN_TILE = 256
HGRN_CHUNK_UNROLL = 2
TAIL_ROW_GROUPS = 2
HGRN_CHUNK = 64


def _dot(a, b):
    return jnp.dot(a, b, preferred_element_type=F32)


def _dot_nt(a, b):
    return lax.dot_general(a, b, (((1,), (1,)), ((), ())), preferred_element_type=F32)


def _dot_tn(a, b):
    return lax.dot_general(a, b, (((0,), (0,)), ((), ())), preferred_element_type=F32)


def _sigmoid(x):
    return 1.0 / (1.0 + jnp.exp(-x))


def _silu(x):
    return x * _sigmoid(x)


def _log_sigmoid(x):
    return jnp.minimum(x, 0.0) - jnp.log(1.0 + jnp.exp(-jnp.abs(x)))


def _gelu_tanh(x):
    return 0.5 * x * (1.0 + jnp.tanh(math.sqrt(2.0 / math.pi) * (x + 0.044715 * (x * x * x))))


def _rms(x, g):
    return x * lax.rsqrt(jnp.mean(x * x, axis=-1, keepdims=True) + EPS) * g


def _prefix_rows(x, op, identity):
    rows, n = x.shape
    row = lax.broadcasted_iota(jnp.int32, (SUBLANES, n), 0)
    outs = []
    carry = None
    for r in range(rows // SUBLANES):
        t = x[r * SUBLANES:(r + 1) * SUBLANES, :]
        for k in (1, 2, 4):
            t = op(t, jnp.where(row >= k, pltpu.roll(t, k, axis=0), identity))
        if carry is not None:
            t = op(t, carry)
        carry = t[SUBLANES - 1:SUBLANES, :]
        outs.append(t)
    return jnp.concatenate(outs, axis=0)


def _cumsum_rows(x):
    return _prefix_rows(x, jnp.add, 0.0)


def _cummax_rows(x):
    return _prefix_rows(x, jnp.maximum, -jnp.inf)


def _params(n_axes):
    return pltpu.CompilerParams(dimension_semantics=("arbitrary",) * n_axes,
                                vmem_limit_bytes=V7X_VMEM_LIMIT_BYTES)


def _const_spec(shape):
    nd = len(shape)
    return pl.BlockSpec(shape, lambda *_: (0,) * nd, pipeline_mode=pl.Buffered(1))


def _rms_matmul_kernel(x_ref, g_ref, w_ref, o_ref):
    h = _rms(x_ref[...], g_ref[...])
    o_ref[...] = _dot(h.astype(BF16), w_ref[...]).astype(o_ref.dtype)


def _rms_matmul(x2d, g, w, tm):
    t, d = x2d.shape
    n = w.shape[1]
    return pl.pallas_call(
        _rms_matmul_kernel,
        grid=(t // tm,),
        in_specs=[pl.BlockSpec((tm, d), lambda i: (i, 0)), _const_spec((1, d)), _const_spec((d, n))],
        out_specs=pl.BlockSpec((tm, n), lambda i: (i, 0)),
        out_shape=jax.ShapeDtypeStruct((t, n), F32),
        compiler_params=_params(1),
        name="rms_proj",
    )(x2d, g.reshape(1, d), w)


def _mlstm_kernel(p_ref, convw_ref, convb_ref, wq_ref, wk_ref, wv_ref, wif_ref, bif_ref, gain_ref, skip_ref,
                  o_ref, xbuf, c_s, m_s, *, heads):
    n_rows, chunk, width = o_ref.shape
    hd = width // heads
    taps = convw_ref.shape[0]
    rows = range(n_rows)

    @pl.when(pl.program_id(0) == 0)
    def _init():
        xbuf[:, 0:SUBLANES, :] = jnp.zeros((n_rows, SUBLANES, width), F32)
        c_s[...] = jnp.zeros_like(c_s)
        m_s[...] = jnp.zeros_like(m_s)

    xm = [p_ref[r, :, 0:width] for r in rows]
    zm = [p_ref[r, :, width:2 * width] for r in rows]

    xc = []
    for r in rows:
        xbuf[r, SUBLANES:SUBLANES + chunk, :] = xm[r]
        conv = convb_ref[...] + convw_ref[taps - 1:taps, :] * xm[r]
        for k in range(taps - 1):
            conv = conv + convw_ref[k:k + 1, :] * xbuf[r, pl.ds(SUBLANES - (taps - 1) + k, chunk), :]
        xbuf[r, 0:SUBLANES, :] = xm[r][chunk - SUBLANES:chunk, :]
        xc.append(_silu(conv))

    q_b, k, k_b, v_b, gates = [], [], [], [], []
    for r in rows:
        xc_b = xc[r].astype(BF16)
        q_b.append(_dot(xc_b, wq_ref[...]).astype(BF16))
        k.append(_dot(xc_b, wk_ref[...]) * (hd ** -0.5))
        k_b.append(k[r].astype(BF16))
        v_b.append(_dot(xm[r].astype(BF16), wv_ref[...]).astype(BF16))
        gates.append(_dot(q_b[r], wif_ref[0:width, :]) + _dot(k_b[r], wif_ref[width:2 * width, :])
                     + _dot(v_b[r], wif_ref[2 * width:3 * width, :]) + bif_ref[...])

    m_prev, row_term, e_neg_m, k_scale, s_old, s_new, g_rows = [], [], [], [], [], [], []
    for r in rows:
        log_i = gates[r][:, 0:LANES]
        b = _cumsum_rows(_log_sigmoid(gates[r][:, LANES:2 * LANES]))
        g = log_i - b
        g_max = _cummax_rows(g)
        m_prev.append(m_s[r, 0:1, :])
        m_t = jnp.maximum(b + m_prev[r], b + g_max)
        row_term.append(b - m_t)
        e_neg_m.append(jnp.exp(-m_t))
        b_last = b[chunk - 1:chunk, :]
        a_max = b_last + g_max[chunk - 1:chunk, :]
        k_scale.append(jnp.exp(g - g_max[chunk - 1:chunk, :]))
        m_new = jnp.maximum(b_last + m_prev[r], a_max)
        s_old.append(jnp.exp(b_last + m_prev[r] - m_new))
        s_new.append(jnp.exp(a_max - m_new))
        m_s[r] = jnp.broadcast_to(m_new, m_s.shape[1:])
        g_rows.append(g.T)

    causal = _causal_mask(chunk)
    ones_b = jnp.ones((chunk, hd), BF16)

    for h in range(heads):
        sl = slice(h * hd, (h + 1) * hd)
        for r in rows:
            qh_b, kh_b = q_b[r][:, sl], k_b[r][:, sl]
            v_ones = jnp.concatenate([v_b[r][:, sl], ones_b], axis=1)
            row_b = jnp.broadcast_to(row_term[r][:, h:h + 1], (chunk, hd))
            row_w = jnp.concatenate([row_b] * (chunk // hd), axis=1)
            w_intra = jnp.exp(jnp.where(causal, row_w + g_rows[r][h:h + 1, :], -jnp.inf))
            s_inter = jnp.exp(row_b + m_prev[r][:, h:h + 1])
            scores = (_dot_nt(qh_b, kh_b) * w_intra).astype(BF16)
            c_prev = c_s[r, h]
            tot = (_dot(scores, v_ones)
                   + jnp.concatenate([s_inter, s_inter], axis=1) * _dot(qh_b, c_prev.astype(BF16)))
            den_floor = jnp.broadcast_to(e_neg_m[r][:, h:h + 1], (chunk, hd))
            hh = tot[:, 0:hd] / jnp.maximum(jnp.abs(tot[:, hd:2 * hd]), den_floor)

            kw = (k[r][:, sl] * jnp.broadcast_to(k_scale[r][:, h:h + 1], (chunk, hd))).astype(BF16)
            c_s[r, h] = s_old[r][:, h:h + 1] * c_prev + s_new[r][:, h:h + 1] * _dot_tn(kw, v_ones)

            mu = jnp.mean(hh, axis=-1, keepdims=True)
            dev = hh - mu
            var = jnp.mean(dev * dev, axis=-1, keepdims=True)
            hn = dev * lax.rsqrt(var + EPS) * gain_ref[:, sl]
            o_ref[r, :, sl] = ((hn + skip_ref[:, sl] * xc[r][:, sl]) * _silu(zm[r][:, sl])).astype(o_ref.dtype)


def _mlstm(proj, conv_w, conv_b, wq, wk, wv, w_if, b_if, gain, skip, heads):
    b, l, _ = proj.shape
    width = conv_w.shape[1]
    hd = width // heads
    chunk = MLSTM_CHUNK
    def block_diag(w):
        nb, d, e = w.shape
        tiled = jnp.tile(w.reshape(nb * d, e), (1, nb))
        same_block = (jnp.arange(nb * d) // d)[:, None] == (jnp.arange(nb * e) // e)[None, :]
        return jnp.where(same_block, tiled, 0.0).astype(BF16)

    wif_pad = (jnp.zeros((w_if.shape[0], 2 * LANES), F32).at[:, :heads].set(w_if[:, :heads])
               .at[:, LANES:LANES + heads].set(w_if[:, heads:])).astype(BF16)
    bif_pad = (jnp.zeros((1, 2 * LANES), F32).at[0, :heads].set(b_if[:heads])
               .at[0, LANES:LANES + heads].set(b_if[heads:]))
    kern = functools.partial(_mlstm_kernel, heads=heads)
    return pl.pallas_call(
        kern,
        grid=(l // chunk,),
        in_specs=[pl.BlockSpec((b, chunk, 2 * width), lambda i: (0, i, 0)),
                  _const_spec(conv_w.shape), _const_spec((1, width)),
                  _const_spec((width, width)), _const_spec((width, width)), _const_spec((width, width)),
                  _const_spec((3 * width, 2 * LANES)), _const_spec((1, 2 * LANES)),
                  _const_spec((1, width)), _const_spec((1, width))],
        out_specs=pl.BlockSpec((b, chunk, width), lambda i: (0, i, 0)),
        out_shape=jax.ShapeDtypeStruct((b, l, width), BF16),
        scratch_shapes=[pltpu.VMEM((b, chunk + SUBLANES, width), F32),
                        pltpu.VMEM((b, heads, hd, 2 * hd), F32),
                        pltpu.VMEM((b, SUBLANES, LANES), F32)],
        compiler_params=_params(1),
        name="mlstm",
    )(proj, conv_w, conv_b.reshape(1, width), block_diag(wq), block_diag(wk), block_diag(wv),
      wif_pad, bif_pad, gain.reshape(1, width), skip.reshape(1, width))


def _permute_rows(pm, x):
    hi = x.astype(BF16)
    rest = x - hi.astype(F32)
    mid = rest.astype(BF16)
    lo = (rest - mid.astype(F32)).astype(BF16)
    return (_dot(pm, hi) + _dot(pm, mid)) + _dot(pm, lo)


def _s5_kernel(u_ref, pm_ref, pmt_ref, wb_ref, cst_ref, wc_ref, d_ref, wglu_ref, bglu_ref, o_ref, xs, carry):
    n_rows, tm, width = o_ref.shape
    seg = tm // SUBLANES
    n_tiles = wb_ref.shape[0]
    kt = wb_ref.shape[1]
    tiles_per_k = n_tiles // (width // kt)
    xw = 2 * LANES
    rows = range(n_rows)

    @pl.when(pl.program_id(0) == 0)
    def _init():
        carry[...] = jnp.zeros_like(carry)

    u = [_permute_rows(pm_ref[...], u_ref[r]) for r in rows]
    for r in rows:
        u_b = u[r].astype(BF16)
        for j in range(n_tiles):
            kk = j // tiles_per_k
            xs[r, :, j * xw:(j + 1) * xw] = _dot(u_b[:, kk * kt:(kk + 1) * kt], wb_ref[j])

    row_id = lax.broadcasted_iota(jnp.int32, (SUBLANES, LANES), 0)

    def recur(r, t, state, tiles, store):
        r8 = t * SUBLANES
        new = []
        for idx, j in enumerate(tiles):
            ar, ai = cst_ref[j, 0], cst_ref[j, 1]
            xr, xi = state[2 * idx], state[2 * idx + 1]
            br = xs[r, pl.ds(r8, SUBLANES), j * xw:j * xw + LANES]
            bi = xs[r, pl.ds(r8, SUBLANES), j * xw + LANES:(j + 1) * xw]
            xr, xi = (ar * xr - ai * xi) + br, (ar * xi + ai * xr) + bi
            if store:
                xs[r, pl.ds(r8, SUBLANES), j * xw:j * xw + LANES] = xr
                xs[r, pl.ds(r8, SUBLANES), j * xw + LANES:(j + 1) * xw] = xi
            new += [xr, xi]
        return new

    for r in rows:
        for j0 in range(0, n_tiles, S5_TILES_PER_PASS):
            tiles = tuple(range(j0, min(j0 + S5_TILES_PER_PASS, n_tiles)))
            state = [jnp.zeros((SUBLANES, LANES), F32) for _ in range(2 * len(tiles))]
            for t in range(seg):
                state = recur(r, t, state, tiles, False)

            starts = []
            for idx, j in enumerate(tiles):
                er, ei = state[2 * idx], state[2 * idx + 1]
                for lvl, k in enumerate((1, 2, 4)):
                    mr, mi = cst_ref[j, 2 + 2 * lvl], cst_ref[j, 3 + 2 * lvl]
                    sr, si = pltpu.roll(er, k, axis=0), pltpu.roll(ei, k, axis=0)
                    er, ei = er + (mr * sr - mi * si), ei + (mr * si + mi * sr)
                pr, pi = cst_ref[j, 8], cst_ref[j, 9]
                cr, ci = carry[r, j, 0], carry[r, j, 1]
                er, ei = er + (pr * cr - pi * ci), ei + (pr * ci + pi * cr)
                carry[r, j, 0] = jnp.broadcast_to(er[SUBLANES - 1:SUBLANES, :], (SUBLANES, LANES))
                carry[r, j, 1] = jnp.broadcast_to(ei[SUBLANES - 1:SUBLANES, :], (SUBLANES, LANES))
                starts.append(jnp.where(row_id == 0, cr, pltpu.roll(er, 1, axis=0)))
                starts.append(jnp.where(row_id == 0, ci, pltpu.roll(ei, 1, axis=0)))
            state = starts
            for t in range(seg):
                state = recur(r, t, state, tiles, True)

    n_out = wc_ref.shape[0]
    kc = wc_ref.shape[1]
    for r in rows:
        y = jnp.concatenate([_dot(xs[r, :, h * kc:(h + 1) * kc].astype(BF16), wc_ref[h]) for h in range(n_out)],
                            axis=1)
        y = _gelu_tanh(y + d_ref[...] * u[r])
        gate = _sigmoid(_dot(y.astype(BF16), wglu_ref[...]) + bglu_ref[...])
        o_ref[r] = _dot(pmt_ref[...], (y * gate).astype(BF16)).astype(o_ref.dtype)


def _s5_weights(a_re, a_im, log_dt, b_re, b_im, c_re, c_im, seg):
    groups, n_state, p = b_re.shape
    width = groups * p
    gpt = LANES // n_state
    n_tiles = groups // gpt
    dt = jnp.exp(log_dt)[:, None]
    mag = jnp.exp(a_re * dt)
    ab_re = mag * jnp.cos(a_im * dt)
    ab_im = mag * jnp.sin(a_im * dt)
    inv = 1.0 / (a_re * a_re + a_im * a_im)
    g_re = ((ab_re - 1.0) * a_re + ab_im * a_im) * inv
    g_im = (ab_im * a_re - (ab_re - 1.0) * a_im) * inv
    bb_re = g_re[..., None] * b_re - g_im[..., None] * b_im
    bb_im = g_re[..., None] * b_im + g_im[..., None] * b_re
    kt = min(MXU_TILE, width)
    n_k = width // kt
    tiles_per_k = n_tiles // n_k
    xw = 2 * LANES

    bb = jnp.stack([bb_re, bb_im], axis=0)
    r_in = jnp.transpose(bb, (1, 3, 0, 2))
    r_in = jnp.broadcast_to(r_in[:, :, :, None, :], (groups, p, 2, gpt, n_state)).reshape(width, xw)
    row_g = jnp.arange(width) // p
    col_gg = (jnp.arange(xw) % LANES) // n_state
    dense_in = jnp.where((row_g % gpt)[:, None] == col_gg[None, :], r_in, 0.0).reshape(n_k, kt, xw)
    tile_id = jnp.arange(n_tiles)
    row_tile = (row_g // gpt).reshape(n_k, kt)[tile_id // tiles_per_k]
    wb = jnp.where(row_tile[:, :, None] == tile_id[:, None, None], dense_in[tile_id // tiles_per_k], 0.0)

    cc = jnp.stack([c_re, -c_im], axis=0)
    r_out = jnp.transpose(cc, (1, 0, 3, 2)).reshape(n_tiles, gpt, 2, n_state, p)
    r_out = jnp.transpose(r_out, (0, 2, 1, 3, 4)).reshape(n_tiles * xw, p)
    r_out = jnp.tile(r_out, (1, kt // p))
    out_row_g = (jnp.arange(n_tiles * xw) // xw) * gpt + (jnp.arange(n_tiles * xw) % LANES) // n_state
    out_col_g = jnp.arange(kt) // p
    wc = jnp.where((out_row_g % (kt // p))[:, None] == out_col_g[None, :], r_out, 0.0)
    wc = wc.reshape(n_k, tiles_per_k * xw, kt)

    a1 = (ab_re.reshape(n_tiles, LANES), ab_im.reshape(n_tiles, LANES))

    def cmul(x, y):
        return x[0] * y[0] - x[1] * y[1], x[0] * y[1] + x[1] * y[0]

    a_seg = a1
    assert seg & (seg - 1) == 0, "segment length must be a power of two"
    for _ in range(seg.bit_length() - 1):
        a_seg = cmul(a_seg, a_seg)
    pows = [a_seg]
    for _ in range(SUBLANES - 1):
        pows.append(cmul(pows[-1], a_seg))
    row = jnp.arange(SUBLANES)[None, :, None]
    planes = [jnp.broadcast_to(a1[c][:, None, :], (n_tiles, SUBLANES, LANES)) for c in range(2)]
    for k in (1, 2, 4):
        for c in range(2):
            planes.append(jnp.where(row >= k, pows[k - 1][c][:, None, :], 0.0))
    for c in range(2):
        planes.append(jnp.stack([pw[c] for pw in pows], axis=1))
    cst = jnp.stack(planes, axis=1).astype(F32)
    return wb.astype(BF16), cst, wc.astype(BF16)


def _s5(proj, col_block, a_re, a_im, log_dt, b_re, b_im, c_re, c_im, d_skip, w_glu, b_glu):
    b, l, _ = proj.shape
    width = d_skip.shape[0]
    tm = S5_TILE
    seg = tm // SUBLANES
    wb, cst, wc = _s5_weights(a_re, a_im, log_dt, b_re, b_im, c_re, c_im, seg)
    n_tiles = wb.shape[0]
    rp = jnp.arange(tm)
    pm = ((rp % SUBLANES) * seg + rp // SUBLANES)[:, None] == rp[None, :]
    return pl.pallas_call(
        _s5_kernel,
        grid=(l // tm,),
        in_specs=[pl.BlockSpec((b, tm, width), lambda i: (0, i, col_block)),
                  _const_spec((tm, tm)), _const_spec((tm, tm)),
                  _const_spec(wb.shape), _const_spec(cst.shape), _const_spec(wc.shape),
                  _const_spec((1, width)), _const_spec((width, width)), _const_spec((1, width))],
        out_specs=pl.BlockSpec((b, tm, width), lambda i: (0, i, 0)),
        out_shape=jax.ShapeDtypeStruct((b, l, width), BF16),
        scratch_shapes=[pltpu.VMEM((b, tm, n_tiles * 2 * LANES), F32),
                        pltpu.VMEM((b, n_tiles, 2, SUBLANES, LANES), F32)],
        compiler_params=_params(1),
        name="s5",
    )(proj, pm.astype(BF16), pm.T.astype(BF16), wb, cst, wc, d_skip.reshape(1, width), w_glu.astype(BF16),
      b_glu.reshape(1, width))


def _hgrn2_chunk(p_ref, r0, lb, gain, st, causal, o_ref, *, heads, chunk):
    n_rows, _, width = o_ref.shape
    e = width // heads
    rows = pl.ds(r0, chunk)
    for h in range(heads):
        sl = slice(h * e, (h + 1) * e)
        lb_h = lb[:, sl]
        for r in range(n_rows):
            q = _silu(p_ref[r, rows, h * e:(h + 1) * e])
            f = p_ref[r, rows, width + h * e:width + (h + 1) * e]
            v_b = p_ref[r, rows, 2 * width + h * e:2 * width + (h + 1) * e].astype(BF16)
            g = p_ref[r, rows, 3 * width + h * e:3 * width + (h + 1) * e]
            fg = lb_h + (1.0 - lb_h) * _sigmoid(f)
            b = _cumsum_rows(jnp.log(fg))
            b_last = b[chunk - 1:chunk, :]
            q_inter = (q * jnp.exp(b)).astype(BF16)
            k_state = ((1.0 - fg) * jnp.exp(b_last - b)).astype(BF16)
            q_intra = (q * jnp.exp(b - b_last)).astype(BF16)
            s_t = st[r, h]
            scores = jnp.where(causal, _dot_nt(q_intra, k_state), 0.0)
            o = _dot(scores.astype(BF16), v_b) + _dot_nt(q_inter, s_t.astype(BF16))
            st[r, h] = s_t * jnp.exp(b_last) + _dot_tn(v_b, k_state)
            o = o * lax.rsqrt(jnp.mean(o * o, axis=-1, keepdims=True) + EPS)
            o_ref[r, rows, sl] = (o * (gain[:, sl] * _silu(g))).astype(o_ref.dtype)


def _causal_mask(chunk):
    ti = lax.broadcasted_iota(jnp.int32, (chunk, chunk), 0)
    si = lax.broadcasted_iota(jnp.int32, (chunk, chunk), 1)
    return si <= ti


def _tail_math(xs, mix_groups, wo_ref, g1_ref, g2_ref, g3_ref, w1_ref, w3_ref, w2_ref):
    mps = []
    for mix_refs in mix_groups:
        mp = None
        row0 = 0
        for m_ref in mix_refs:
            part = _dot(m_ref[...], wo_ref[row0:row0 + m_ref.shape[1], :])
            mp = part if mp is None else mp + part
            row0 += m_ref.shape[1]
        mps.append(mp)
    x1s = [x + _rms(mp, g1_ref[...]) for x, mp in zip(xs, mps)]
    hs = [_rms(x1, g2_ref[...]).astype(BF16) for x1 in x1s]
    ys = []
    for h in hs:
        z = (_silu(_dot(h, w1_ref[...])) * _dot(h, w3_ref[...])).astype(BF16)
        ys.append(_dot(z, w2_ref[...]))
    return [x1 + _rms(y, g3_ref[...]) for x1, y in zip(x1s, ys)]


def _hgrn2_kernel(p_ref, lb_ref, gain_ref, o_ref, st, *, heads, chunk):
    tm = o_ref.shape[1]

    @pl.when(pl.program_id(0) == 0)
    def _init():
        st[...] = jnp.zeros_like(st)

    lb = lb_ref[...]
    gain = gain_ref[...]
    causal = _causal_mask(chunk)

    def body(c, _):
        _hgrn2_chunk(p_ref, pl.multiple_of(c * chunk, chunk), lb, gain, st, causal, o_ref, heads=heads, chunk=chunk)
        return 0

    lax.fori_loop(0, tm // chunk, body, 0, unroll=HGRN_CHUNK_UNROLL)


def _hgrn2(proj, lb, gain):
    b, l, w4 = proj.shape
    width = w4 // 4
    heads = width // H_EXPAND
    tm = HGRN_TILE
    kern = functools.partial(_hgrn2_kernel, heads=heads, chunk=HGRN_CHUNK)
    return pl.pallas_call(
        kern,
        grid=(l // tm,),
        in_specs=[pl.BlockSpec((b, tm, w4), lambda i: (0, i, 0)),
                  _const_spec((1, width)), _const_spec((1, width))],
        out_specs=pl.BlockSpec((b, tm, width), lambda i: (0, i, 0)),
        out_shape=jax.ShapeDtypeStruct((b, l, width), BF16),
        scratch_shapes=[pltpu.VMEM((b, heads, H_EXPAND, H_EXPAND), F32)],
        compiler_params=_params(1),
        name="hgrn2",
    )(proj, lb.reshape(1, width), gain.reshape(1, width))


def _tail_kernel(*refs, n_mix):
    x_ref = refs[0]
    mix_refs = refs[1:1 + n_mix]
    wo_ref, g1_ref, g2_ref, g3_ref, w1_ref, w3_ref, w2_ref, o_ref = refs[1 + n_mix:]
    rows = o_ref.shape[0] // TAIL_ROW_GROUPS
    groups = [slice(r * rows, (r + 1) * rows) for r in range(TAIL_ROW_GROUPS)]
    outs = _tail_math([x_ref[sl, :] for sl in groups], [[m.at[sl, :] for m in mix_refs] for sl in groups],
                      wo_ref, g1_ref, g2_ref, g3_ref, w1_ref, w3_ref, w2_ref)
    for sl, out in zip(groups, outs):
        o_ref[sl, :] = out


def _block_tail(x2d, mixes, w_out, g1, g2, g3, w1, w3, w2):
    t, d = x2d.shape
    f = w1.shape[1]
    tm = ROW_TILE_TAIL
    n_mix = len(mixes)
    row = lambda i: (i, 0)
    in_specs = ([pl.BlockSpec((tm, d), row)]
                + [pl.BlockSpec((tm, m.shape[1]), row) for m in mixes]
                + [_const_spec(w_out.shape)]
                + [_const_spec((1, d))] * 3
                + [_const_spec((d, f)), _const_spec((d, f)), _const_spec((f, d))])
    return pl.pallas_call(
        functools.partial(_tail_kernel, n_mix=n_mix),
        grid=(t // tm,),
        in_specs=in_specs,
        out_specs=pl.BlockSpec((tm, d), row),
        out_shape=jax.ShapeDtypeStruct((t, d), F32),
        compiler_params=_params(1),
        name="outproj_swiglu",
    )(x2d, *mixes, w_out.astype(BF16), g1.reshape(1, d), g2.reshape(1, d), g3.reshape(1, d),
      w1.astype(BF16), w3.astype(BF16), w2.astype(BF16))


def kernel(x, norm_g, ab_w_in, ab_conv_w, ab_conv_b, ab_wq, ab_wk, ab_wv, ab_w_if, ab_b_if, ab_mh_gain, ab_skip,
           ab_a_re, ab_a_im, ab_log_dt, ab_b_re, ab_b_im, ab_c_re, ab_c_im, ab_d, ab_w_glu, ab_b_glu, ab_w_out,
           c_w_in, c_lb_raw, c_g_gain, c_w_out, ffn_w1, ffn_w3, ffn_w2):
    bsz, seq, d = x.shape
    t = bsz * seq
    depth = norm_g.shape[0]
    x2d = x.reshape(t, d)
    lbs = jnp.cumsum(jax.nn.softmax(c_lb_raw.astype(F32), axis=0), axis=0)
    for layer in range(depth):
        j = layer // 2
        if layer % 2 == 0:
            m_width = ab_conv_w.shape[2]
            s_width = ab_d.shape[1]
            proj = _rms_matmul(x2d, norm_g[layer, 0], ab_w_in[j].astype(BF16), ROW_TILE_PROJ).reshape(bsz, seq, -1)
            out_m = _mlstm(proj, ab_conv_w[j], ab_conv_b[j], ab_wq[j], ab_wk[j], ab_wv[j], ab_w_if[j], ab_b_if[j],
                           ab_mh_gain[j], ab_skip[j], M_HEADS)
            out_s = _s5(proj, (2 * m_width) // s_width, ab_a_re[j], ab_a_im[j], ab_log_dt[j], ab_b_re[j], ab_b_im[j],
                        ab_c_re[j], ab_c_im[j], ab_d[j], ab_w_glu[j], ab_b_glu[j])
            mixes = [out_m.reshape(t, m_width), out_s.reshape(t, s_width)]
            w_out = ab_w_out[j]
        else:
            proj = _rms_matmul(x2d, norm_g[layer, 0], c_w_in[j].astype(BF16), ROW_TILE_PROJ).reshape(bsz, seq, -1)
            lb = lbs[layer] - lbs[0]
            mixes = [_hgrn2(proj, lb, c_g_gain[j]).reshape(t, -1)]
            w_out = c_w_out[j]
        x2d = _block_tail(x2d, mixes, w_out, norm_g[layer, 1], norm_g[layer, 2], norm_g[layer, 3],
                          ffn_w1[layer], ffn_w3[layer], ffn_w2[layer])
    return x2d.reshape(bsz, seq, d)
```

```python
import functools
import math

import jax
import jax.numpy as jnp
from jax import lax
from jax.experimental import pallas as pl
from jax.experimental.pallas import tpu as pltpu

F32 = jnp.float32
BF16 = jnp.bfloat16
EPS = 1e-6

V7X_VMEM_LIMIT_BYTES = 56 * 1024 * 1024
SUBLANES = 8
LANES = 128
MXU_TILE = 256

M_HEADS = 4
M_CONV = 4
QKV_BLOCK = 4
S_GROUP = 16
S_STATE = 64
H_EXPAND = 128

ROW_TILE_PROJ = 512
ROW_TILE_TAIL = 512
MLSTM_CHUNK = 256
S5_TILE = 256
S5_TILES_PER_PASS = 8
HGRN_TILE = 256
HGRN_CHUNK_UNROLL = 2
TAIL_ROW_GROUPS = 2
HGRN_CHUNK = 64


def _dot(a, b):
    return jnp.dot(a, b, preferred_element_type=F32)


def _dot_nt(a, b):
    return lax.dot_general(a, b, (((1,), (1,)), ((), ())), preferred_element_type=F32)


def _dot_tn(a, b):
    return lax.dot_general(a, b, (((0,), (0,)), ((), ())), preferred_element_type=F32)


def _sigmoid(x):
    return 1.0 / (1.0 + jnp.exp(-x))


def _silu(x):
    return x * _sigmoid(x)


def _log_sigmoid(x):
    return jnp.minimum(x, 0.0) - jnp.log(1.0 + jnp.exp(-jnp.abs(x)))


def _gelu_tanh(x):
    return 0.5 * x * (1.0 + jnp.tanh(math.sqrt(2.0 / math.pi) * (x + 0.044715 * (x * x * x))))


def _rms(x, g):
    return x * lax.rsqrt(jnp.mean(x * x, axis=-1, keepdims=True) + EPS) * g


def _prefix_rows(x, op, identity):
    rows, n = x.shape
    row = lax.broadcasted_iota(jnp.int32, (SUBLANES, n), 0)
    outs = []
    carry = None
    for r in range(rows // SUBLANES):
        t = x[r * SUBLANES:(r + 1) * SUBLANES, :]
        for k in (1, 2, 4):
            t = op(t, jnp.where(row >= k, pltpu.roll(t, k, axis=0), identity))
        if carry is not None:
            t = op(t, carry)
        carry = t[SUBLANES - 1:SUBLANES, :]
        outs.append(t)
    return jnp.concatenate(outs, axis=0)


def _cumsum_rows(x):
    return _prefix_rows(x, jnp.add, 0.0)


def _cummax_rows(x):
    return _prefix_rows(x, jnp.maximum, -jnp.inf)


def _params(n_axes):
    return pltpu.CompilerParams(dimension_semantics=("arbitrary",) * n_axes,
                                vmem_limit_bytes=V7X_VMEM_LIMIT_BYTES)


def _const_spec(shape):
    nd = len(shape)
    return pl.BlockSpec(shape, lambda *_: (0,) * nd, pipeline_mode=pl.Buffered(1))


def _rms_matmul_kernel(x_ref, g_ref, w_ref, o_ref):
    h = _rms(x_ref[...], g_ref[...])
    o_ref[...] = _dot(h.astype(BF16), w_ref[...]).astype(o_ref.dtype)


def _rms_matmul(x2d, g, w, tm):
    t, d = x2d.shape
    n = w.shape[1]
    return pl.pallas_call(
        _rms_matmul_kernel,
        grid=(t // tm,),
        in_specs=[pl.BlockSpec((tm, d), lambda i: (i, 0)), _const_spec((1, d)), _const_spec((d, n))],
        out_specs=pl.BlockSpec((tm, n), lambda i: (i, 0)),
        out_shape=jax.ShapeDtypeStruct((t, n), F32),
        compiler_params=_params(1),
        name="rms_proj",
    )(x2d, g.reshape(1, d), w)


def _mlstm_kernel(x_ref, ng_ref, win_ref, convw_ref, convb_ref, wq_ref, wk_ref, wv_ref, wif_ref, bif_ref, gain_ref,
                  skip_ref, o_ref, xbuf, c_s, m_s, *, heads):
    n_rows, chunk, width = o_ref.shape
    hd = width // heads
    taps = convw_ref.shape[0]
    rows = range(n_rows)

    @pl.when(pl.program_id(0) == 0)
    def _init():
        xbuf[:, 0:SUBLANES, :] = jnp.zeros((n_rows, SUBLANES, width), F32)
        c_s[...] = jnp.zeros_like(c_s)
        m_s[...] = jnp.zeros_like(m_s)

    proj = [_dot(_rms(x_ref[r], ng_ref[...]).astype(BF16), win_ref[...]) for r in rows]
    xm = [p[:, 0:width] for p in proj]
    zm = [p[:, width:2 * width] for p in proj]

    xc = []
    for r in rows:
        xbuf[r, SUBLANES:SUBLANES + chunk, :] = xm[r]
        conv = convb_ref[...] + convw_ref[taps - 1:taps, :] * xm[r]
        for k in range(taps - 1):
            conv = conv + convw_ref[k:k + 1, :] * xbuf[r, pl.ds(SUBLANES - (taps - 1) + k, chunk), :]
        xbuf[r, 0:SUBLANES, :] = xm[r][chunk - SUBLANES:chunk, :]
        xc.append(_silu(conv))

    q_b, k, k_b, v_b, gates = [], [], [], [], []
    for r in rows:
        xc_b = xc[r].astype(BF16)
        q_b.append(_dot(xc_b, wq_ref[...]).astype(BF16))
        k.append(_dot(xc_b, wk_ref[...]) * (hd ** -0.5))
        k_b.append(k[r].astype(BF16))
        v_b.append(_dot(xm[r].astype(BF16), wv_ref[...]).astype(BF16))
        gates.append(_dot(q_b[r], wif_ref[0:width, :]) + _dot(k_b[r], wif_ref[width:2 * width, :])
                     + _dot(v_b[r], wif_ref[2 * width:3 * width, :]) + bif_ref[...])

    m_prev, row_term, e_neg_m, k_scale, s_old, s_new, g_rows = [], [], [], [], [], [], []
    for r in rows:
        log_i = gates[r][:, 0:LANES]
        b = _cumsum_rows(_log_sigmoid(gates[r][:, LANES:2 * LANES]))
        g = log_i - b
        g_max = _cummax_rows(g)
        m_prev.append(m_s[r, 0:1, :])
        m_t = jnp.maximum(b + m_prev[r], b + g_max)
        row_term.append(b - m_t)
        e_neg_m.append(jnp.exp(-m_t))
        b_last = b[chunk - 1:chunk, :]
        a_max = b_last + g_max[chunk - 1:chunk, :]
        k_scale.append(jnp.exp(g - g_max[chunk - 1:chunk, :]))
        m_new = jnp.maximum(b_last + m_prev[r], a_max)
        s_old.append(jnp.exp(b_last + m_prev[r] - m_new))
        s_new.append(jnp.exp(a_max - m_new))
        m_s[r] = jnp.broadcast_to(m_new, m_s.shape[1:])
        g_rows.append(g.T)

    causal = _causal_mask(chunk)
    ones_b = jnp.ones((chunk, hd), BF16)

    for h in range(heads):
        sl = slice(h * hd, (h + 1) * hd)
        for r in rows:
            qh_b, kh_b = q_b[r][:, sl], k_b[r][:, sl]
            v_ones = jnp.concatenate([v_b[r][:, sl], ones_b], axis=1)
            row_b = jnp.broadcast_to(row_term[r][:, h:h + 1], (chunk, hd))
            row_w = jnp.concatenate([row_b] * (chunk // hd), axis=1)
            w_intra = jnp.exp(jnp.where(causal, row_w + g_rows[r][h:h + 1, :], -jnp.inf))
            s_inter = jnp.exp(row_b + m_prev[r][:, h:h + 1])
            scores = (_dot_nt(qh_b, kh_b) * w_intra).astype(BF16)
            c_prev = c_s[r, h]
            tot = (_dot(scores, v_ones)
                   + jnp.concatenate([s_inter, s_inter], axis=1) * _dot(qh_b, c_prev.astype(BF16)))
            den_floor = jnp.broadcast_to(e_neg_m[r][:, h:h + 1], (chunk, hd))
            hh = tot[:, 0:hd] / jnp.maximum(jnp.abs(tot[:, hd:2 * hd]), den_floor)

            kw = (k[r][:, sl] * jnp.broadcast_to(k_scale[r][:, h:h + 1], (chunk, hd))).astype(BF16)
            c_s[r, h] = s_old[r][:, h:h + 1] * c_prev + s_new[r][:, h:h + 1] * _dot_tn(kw, v_ones)

            mu = jnp.mean(hh, axis=-1, keepdims=True)
            dev = hh - mu
            var = jnp.mean(dev * dev, axis=-1, keepdims=True)
            hn = dev * lax.rsqrt(var + EPS) * gain_ref[:, sl]
            o_ref[r, :, sl] = ((hn + skip_ref[:, sl] * xc[r][:, sl]) * _silu(zm[r][:, sl])).astype(o_ref.dtype)


def _mlstm(x, norm_gain, w_in, conv_w, conv_b, wq, wk, wv, w_if, b_if, gain, skip, heads):
    b, l, d = x.shape
    width = conv_w.shape[1]
    hd = width // heads
    chunk = MLSTM_CHUNK
    def block_diag(w):
        nb, d, e = w.shape
        tiled = jnp.tile(w.reshape(nb * d, e), (1, nb))
        same_block = (jnp.arange(nb * d) // d)[:, None] == (jnp.arange(nb * e) // e)[None, :]
        return jnp.where(same_block, tiled, 0.0).astype(BF16)

    wif_pad = (jnp.zeros((w_if.shape[0], 2 * LANES), F32).at[:, :heads].set(w_if[:, :heads])
               .at[:, LANES:LANES + heads].set(w_if[:, heads:])).astype(BF16)
    bif_pad = (jnp.zeros((1, 2 * LANES), F32).at[0, :heads].set(b_if[:heads])
               .at[0, LANES:LANES + heads].set(b_if[heads:]))
    kern = functools.partial(_mlstm_kernel, heads=heads)
    return pl.pallas_call(
        kern,
        grid=(l // chunk,),
        in_specs=[pl.BlockSpec((b, chunk, d), lambda i: (0, i, 0)), _const_spec((1, d)), _const_spec(w_in.shape),
                  _const_spec(conv_w.shape), _const_spec((1, width)),
                  _const_spec((width, width)), _const_spec((width, width)), _const_spec((width, width)),
                  _const_spec((3 * width, 2 * LANES)), _const_spec((1, 2 * LANES)),
                  _const_spec((1, width)), _const_spec((1, width))],
        out_specs=pl.BlockSpec((b, chunk, width), lambda i: (0, i, 0)),
        out_shape=jax.ShapeDtypeStruct((b, l, width), BF16),
        scratch_shapes=[pltpu.VMEM((b, chunk + SUBLANES, width), F32),
                        pltpu.VMEM((b, heads, hd, 2 * hd), F32),
                        pltpu.VMEM((b, SUBLANES, LANES), F32)],
        compiler_params=_params(1),
        name="mlstm",
    )(x, norm_gain.reshape(1, d), w_in.astype(BF16), conv_w, conv_b.reshape(1, width),
      block_diag(wq), block_diag(wk), block_diag(wv),
      wif_pad, bif_pad, gain.reshape(1, width), skip.reshape(1, width))


def _s5_kernel(x_ref, ng_ref, win_ref, pm_ref, pmt_ref, wb_ref, cst_ref, wc_ref, d_ref, wglu_ref, bglu_ref,
               o_ref, xs, carry):
    n_rows, tm, width = o_ref.shape
    seg = tm // SUBLANES
    n_tiles = wb_ref.shape[0]
    kt = wb_ref.shape[1]
    tiles_per_k = n_tiles // (width // kt)
    xw = 2 * LANES
    rows = range(n_rows)

    @pl.when(pl.program_id(0) == 0)
    def _init():
        carry[...] = jnp.zeros_like(carry)

    u = []
    for r in rows:
        h_b = _rms(x_ref[r], ng_ref[...]).astype(BF16)
        u.append(_dot(_dot(pm_ref[...], h_b).astype(BF16), win_ref[...]))
    for r in rows:
        u_b = u[r].astype(BF16)
        for j in range(n_tiles):
            kk = j // tiles_per_k
            xs[r, :, j * xw:(j + 1) * xw] = _dot(u_b[:, kk * kt:(kk + 1) * kt], wb_ref[j])

    row_id = lax.broadcasted_iota(jnp.int32, (SUBLANES, LANES), 0)

    def recur(r, t, state, tiles, store):
        r8 = t * SUBLANES
        new = []
        for idx, j in enumerate(tiles):
            ar, ai = cst_ref[j, 0], cst_ref[j, 1]
            xr, xi = state[2 * idx], state[2 * idx + 1]
            br = xs[r, pl.ds(r8, SUBLANES), j * xw:j * xw + LANES]
            bi = xs[r, pl.ds(r8, SUBLANES), j * xw + LANES:(j + 1) * xw]
            xr, xi = (ar * xr - ai * xi) + br, (ar * xi + ai * xr) + bi
            if store:
                xs[r, pl.ds(r8, SUBLANES), j * xw:j * xw + LANES] = xr
                xs[r, pl.ds(r8, SUBLANES), j * xw + LANES:(j + 1) * xw] = xi
            new += [xr, xi]
        return new

    for r in rows:
        for j0 in range(0, n_tiles, S5_TILES_PER_PASS):
            tiles = tuple(range(j0, min(j0 + S5_TILES_PER_PASS, n_tiles)))
            state = [jnp.zeros((SUBLANES, LANES), F32) for _ in range(2 * len(tiles))]
            for t in range(seg):
                state = recur(r, t, state, tiles, False)

            starts = []
            for idx, j in enumerate(tiles):
                er, ei = state[2 * idx], state[2 * idx + 1]
                for lvl, k in enumerate((1, 2, 4)):
                    mr, mi = cst_ref[j, 2 + 2 * lvl], cst_ref[j, 3 + 2 * lvl]
                    sr, si = pltpu.roll(er, k, axis=0), pltpu.roll(ei, k, axis=0)
                    er, ei = er + (mr * sr - mi * si), ei + (mr * si + mi * sr)
                pr, pi = cst_ref[j, 8], cst_ref[j, 9]
                cr, ci = carry[r, j, 0], carry[r, j, 1]
                er, ei = er + (pr * cr - pi * ci), ei + (pr * ci + pi * cr)
                carry[r, j, 0] = jnp.broadcast_to(er[SUBLANES - 1:SUBLANES, :], (SUBLANES, LANES))
                carry[r, j, 1] = jnp.broadcast_to(ei[SUBLANES - 1:SUBLANES, :], (SUBLANES, LANES))
                starts.append(jnp.where(row_id == 0, cr, pltpu.roll(er, 1, axis=0)))
                starts.append(jnp.where(row_id == 0, ci, pltpu.roll(ei, 1, axis=0)))
            state = starts
            for t in range(seg):
                state = recur(r, t, state, tiles, True)

    n_out = wc_ref.shape[0]
    kc = wc_ref.shape[1]
    for r in rows:
        y = jnp.concatenate([_dot(xs[r, :, h * kc:(h + 1) * kc].astype(BF16), wc_ref[h]) for h in range(n_out)],
                            axis=1)
        y = _gelu_tanh(y + d_ref[...] * u[r])
        gate = _sigmoid(_dot(y.astype(BF16), wglu_ref[...]) + bglu_ref[...])
        o_ref[r] = _dot(pmt_ref[...], (y * gate).astype(BF16)).astype(o_ref.dtype)


def _s5_weights(a_re, a_im, log_dt, b_re, b_im, c_re, c_im, seg):
    groups, n_state, p = b_re.shape
    width = groups * p
    gpt = LANES // n_state
    n_tiles = groups // gpt
    dt = jnp.exp(log_dt)[:, None]
    mag = jnp.exp(a_re * dt)
    ab_re = mag * jnp.cos(a_im * dt)
    ab_im = mag * jnp.sin(a_im * dt)
    inv = 1.0 / (a_re * a_re + a_im * a_im)
    g_re = ((ab_re - 1.0) * a_re + ab_im * a_im) * inv
    g_im = (ab_im * a_re - (ab_re - 1.0) * a_im) * inv
    bb_re = g_re[..., None] * b_re - g_im[..., None] * b_im
    bb_im = g_re[..., None] * b_im + g_im[..., None] * b_re
    kt = min(MXU_TILE, width)
    n_k = width // kt
    tiles_per_k = n_tiles // n_k
    xw = 2 * LANES

    bb = jnp.stack([bb_re, bb_im], axis=0)
    r_in = jnp.transpose(bb, (1, 3, 0, 2))
    r_in = jnp.broadcast_to(r_in[:, :, :, None, :], (groups, p, 2, gpt, n_state)).reshape(width, xw)
    row_g = jnp.arange(width) // p
    col_gg = (jnp.arange(xw) % LANES) // n_state
    dense_in = jnp.where((row_g % gpt)[:, None] == col_gg[None, :], r_in, 0.0).reshape(n_k, kt, xw)
    tile_id = jnp.arange(n_tiles)
    row_tile = (row_g // gpt).reshape(n_k, kt)[tile_id // tiles_per_k]
    wb = jnp.where(row_tile[:, :, None] == tile_id[:, None, None], dense_in[tile_id // tiles_per_k], 0.0)

    cc = jnp.stack([c_re, -c_im], axis=0)
    r_out = jnp.transpose(cc, (1, 0, 3, 2)).reshape(n_tiles, gpt, 2, n_state, p)
    r_out = jnp.transpose(r_out, (0, 2, 1, 3, 4)).reshape(n_tiles * xw, p)
    r_out = jnp.tile(r_out, (1, kt // p))
    out_row_g = (jnp.arange(n_tiles * xw) // xw) * gpt + (jnp.arange(n_tiles * xw) % LANES) // n_state
    out_col_g = jnp.arange(kt) // p
    wc = jnp.where((out_row_g % (kt // p))[:, None] == out_col_g[None, :], r_out, 0.0)
    wc = wc.reshape(n_k, tiles_per_k * xw, kt)

    a1 = (ab_re.reshape(n_tiles, LANES), ab_im.reshape(n_tiles, LANES))

    def cmul(x, y):
        return x[0] * y[0] - x[1] * y[1], x[0] * y[1] + x[1] * y[0]

    a_seg = a1
    assert seg & (seg - 1) == 0, "segment length must be a power of two"
    for _ in range(seg.bit_length() - 1):
        a_seg = cmul(a_seg, a_seg)
    pows = [a_seg]
    for _ in range(SUBLANES - 1):
        pows.append(cmul(pows[-1], a_seg))
    row = jnp.arange(SUBLANES)[None, :, None]
    planes = [jnp.broadcast_to(a1[c][:, None, :], (n_tiles, SUBLANES, LANES)) for c in range(2)]
    for k in (1, 2, 4):
        for c in range(2):
            planes.append(jnp.where(row >= k, pows[k - 1][c][:, None, :], 0.0))
    for c in range(2):
        planes.append(jnp.stack([pw[c] for pw in pows], axis=1))
    cst = jnp.stack(planes, axis=1).astype(F32)
    return wb.astype(BF16), cst, wc.astype(BF16)


def _s5(x, norm_gain, w_in, a_re, a_im, log_dt, b_re, b_im, c_re, c_im, d_skip, w_glu, b_glu):
    b, l, d = x.shape
    width = d_skip.shape[0]
    tm = S5_TILE
    seg = tm // SUBLANES
    wb, cst, wc = _s5_weights(a_re, a_im, log_dt, b_re, b_im, c_re, c_im, seg)
    n_tiles = wb.shape[0]
    rp = jnp.arange(tm)
    pm = ((rp % SUBLANES) * seg + rp // SUBLANES)[:, None] == rp[None, :]
    return pl.pallas_call(
        _s5_kernel,
        grid=(l // tm,),
        in_specs=[pl.BlockSpec((b, tm, d), lambda i: (0, i, 0)), _const_spec((1, d)), _const_spec(w_in.shape),
                  _const_spec((tm, tm)), _const_spec((tm, tm)),
                  _const_spec(wb.shape), _const_spec(cst.shape), _const_spec(wc.shape),
                  _const_spec((1, width)), _const_spec((width, width)), _const_spec((1, width))],
        out_specs=pl.BlockSpec((b, tm, width), lambda i: (0, i, 0)),
        out_shape=jax.ShapeDtypeStruct((b, l, width), BF16),
        scratch_shapes=[pltpu.VMEM((b, tm, n_tiles * 2 * LANES), F32),
                        pltpu.VMEM((b, n_tiles, 2, SUBLANES, LANES), F32)],
        compiler_params=_params(1),
        name="s5",
    )(x, norm_gain.reshape(1, d), w_in.astype(BF16), pm.astype(BF16), pm.T.astype(BF16), wb, cst, wc,
      d_skip.reshape(1, width), w_glu.astype(BF16),
      b_glu.reshape(1, width))


def _hgrn2_chunk(p_ref, r0, lb, gain, st, causal, o_ref, *, heads, chunk):
    n_rows, _, width = o_ref.shape
    e = width // heads
    rows = pl.ds(r0, chunk)
    for h in range(heads):
        sl = slice(h * e, (h + 1) * e)
        lb_h = lb[:, sl]
        for r in range(n_rows):
            q = _silu(p_ref[r, rows, h * e:(h + 1) * e])
            f = p_ref[r, rows, width + h * e:width + (h + 1) * e]
            v_b = p_ref[r, rows, 2 * width + h * e:2 * width + (h + 1) * e].astype(BF16)
            g = p_ref[r, rows, 3 * width + h * e:3 * width + (h + 1) * e]
            fg = lb_h + (1.0 - lb_h) * _sigmoid(f)
            b = _cumsum_rows(jnp.log(fg))
            b_last = b[chunk - 1:chunk, :]
            q_inter = (q * jnp.exp(b)).astype(BF16)
            k_state = ((1.0 - fg) * jnp.exp(b_last - b)).astype(BF16)
            q_intra = (q * jnp.exp(b - b_last)).astype(BF16)
            s_t = st[r, h]
            scores = jnp.where(causal, _dot_nt(q_intra, k_state), 0.0)
            o = _dot(scores.astype(BF16), v_b) + _dot_nt(q_inter, s_t.astype(BF16))
            st[r, h] = s_t * jnp.exp(b_last) + _dot_tn(v_b, k_state)
            o = o * lax.rsqrt(jnp.mean(o * o, axis=-1, keepdims=True) + EPS)
            o_ref[r, rows, sl] = (o * (gain[:, sl] * _silu(g))).astype(o_ref.dtype)


def _causal_mask(chunk):
    ti = lax.broadcasted_iota(jnp.int32, (chunk, chunk), 0)
    si = lax.broadcasted_iota(jnp.int32, (chunk, chunk), 1)
    return si <= ti


def _tail_math(xs, mix_groups, wo_ref, g1_ref, g2_ref, g3_ref, w1_ref, w3_ref, w2_ref):
    mps = []
    for mix_refs in mix_groups:
        mp = None
        row0 = 0
        for m_ref in mix_refs:
            part = _dot(m_ref[...], wo_ref[row0:row0 + m_ref.shape[1], :])
            mp = part if mp is None else mp + part
            row0 += m_ref.shape[1]
        mps.append(mp)
    x1s = [x + _rms(mp, g1_ref[...]) for x, mp in zip(xs, mps)]
    hs = [_rms(x1, g2_ref[...]).astype(BF16) for x1 in x1s]
    ys = []
    for h in hs:
        z = (_silu(_dot(h, w1_ref[...])) * _dot(h, w3_ref[...])).astype(BF16)
        ys.append(_dot(z, w2_ref[...]))
    return [x1 + _rms(y, g3_ref[...]) for x1, y in zip(x1s, ys)]


def _hgrn2_kernel(p_ref, lb_ref, gain_ref, o_ref, st, *, heads, chunk):
    tm = o_ref.shape[1]

    @pl.when(pl.program_id(0) == 0)
    def _init():
        st[...] = jnp.zeros_like(st)

    lb = lb_ref[...]
    gain = gain_ref[...]
    causal = _causal_mask(chunk)

    def body(c, _):
        _hgrn2_chunk(p_ref, pl.multiple_of(c * chunk, chunk), lb, gain, st, causal, o_ref, heads=heads, chunk=chunk)
        return 0

    lax.fori_loop(0, tm // chunk, body, 0, unroll=HGRN_CHUNK_UNROLL)


def _hgrn2(proj, lb, gain):
    b, l, w4 = proj.shape
    width = w4 // 4
    heads = width // H_EXPAND
    tm = HGRN_TILE
    kern = functools.partial(_hgrn2_kernel, heads=heads, chunk=HGRN_CHUNK)
    return pl.pallas_call(
        kern,
        grid=(l // tm,),
        in_specs=[pl.BlockSpec((b, tm, w4), lambda i: (0, i, 0)),
                  _const_spec((1, width)), _const_spec((1, width))],
        out_specs=pl.BlockSpec((b, tm, width), lambda i: (0, i, 0)),
        out_shape=jax.ShapeDtypeStruct((b, l, width), BF16),
        scratch_shapes=[pltpu.VMEM((b, heads, H_EXPAND, H_EXPAND), F32)],
        compiler_params=_params(1),
        name="hgrn2",
    )(proj, lb.reshape(1, width), gain.reshape(1, width))


def _tail_kernel(*refs, n_mix):
    x_ref = refs[0]
    mix_refs = refs[1:1 + n_mix]
    wo_ref, g1_ref, g2_ref, g3_ref, w1_ref, w3_ref, w2_ref, o_ref = refs[1 + n_mix:]
    rows = o_ref.shape[0] // TAIL_ROW_GROUPS
    groups = [slice(r * rows, (r + 1) * rows) for r in range(TAIL_ROW_GROUPS)]
    outs = _tail_math([x_ref[sl, :] for sl in groups], [[m.at[sl, :] for m in mix_refs] for sl in groups],
                      wo_ref, g1_ref, g2_ref, g3_ref, w1_ref, w3_ref, w2_ref)
    for sl, out in zip(groups, outs):
        o_ref[sl, :] = out


def _block_tail(x2d, mixes, w_out, g1, g2, g3, w1, w3, w2):
    t, d = x2d.shape
    f = w1.shape[1]
    tm = ROW_TILE_TAIL
    n_mix = len(mixes)
    row = lambda i: (i, 0)
    in_specs = ([pl.BlockSpec((tm, d), row)]
                + [pl.BlockSpec((tm, m.shape[1]), row) for m in mixes]
                + [_const_spec(w_out.shape)]
                + [_const_spec((1, d))] * 3
                + [_const_spec((d, f)), _const_spec((d, f)), _const_spec((f, d))])
    return pl.pallas_call(
        functools.partial(_tail_kernel, n_mix=n_mix),
        grid=(t // tm,),
        in_specs=in_specs,
        out_specs=pl.BlockSpec((tm, d), row),
        out_shape=jax.ShapeDtypeStruct((t, d), F32),
        compiler_params=_params(1),
        name="outproj_swiglu",
    )(x2d, *mixes, w_out.astype(BF16), g1.reshape(1, d), g2.reshape(1, d), g3.reshape(1, d),
      w1.astype(BF16), w3.astype(BF16), w2.astype(BF16))


def kernel(x, norm_g, ab_w_in, ab_conv_w, ab_conv_b, ab_wq, ab_wk, ab_wv, ab_w_if, ab_b_if, ab_mh_gain, ab_skip,
           ab_a_re, ab_a_im, ab_log_dt, ab_b_re, ab_b_im, ab_c_re, ab_c_im, ab_d, ab_w_glu, ab_b_glu, ab_w_out,
           c_w_in, c_lb_raw, c_g_gain, c_w_out, ffn_w1, ffn_w3, ffn_w2):
    bsz, seq, d = x.shape
    t = bsz * seq
    depth = norm_g.shape[0]
    x2d = x.reshape(t, d)
    lbs = jnp.cumsum(jax.nn.softmax(c_lb_raw.astype(F32), axis=0), axis=0)
    for layer in range(depth):
        j = layer // 2
        if layer % 2 == 0:
            m_width = ab_conv_w.shape[2]
            s_width = ab_d.shape[1]
            x3d = x2d.reshape(bsz, seq, d)
            out_m = _mlstm(x3d, norm_g[layer, 0], ab_w_in[j][:, :2 * m_width], ab_conv_w[j], ab_conv_b[j], ab_wq[j],
                           ab_wk[j], ab_wv[j], ab_w_if[j], ab_b_if[j], ab_mh_gain[j], ab_skip[j], M_HEADS)
            out_s = _s5(x3d, norm_g[layer, 0], ab_w_in[j][:, 2 * m_width:], ab_a_re[j], ab_a_im[j], ab_log_dt[j],
                        ab_b_re[j], ab_b_im[j], ab_c_re[j], ab_c_im[j], ab_d[j], ab_w_glu[j], ab_b_glu[j])
            mixes = [out_m.reshape(t, m_width), out_s.reshape(t, s_width)]
            w_out = ab_w_out[j]
        else:
            proj = _rms_matmul(x2d, norm_g[layer, 0], c_w_in[j].astype(BF16), ROW_TILE_PROJ).reshape(bsz, seq, -1)
            lb = lbs[layer] - lbs[0]
            mixes = [_hgrn2(proj, lb, c_g_gain[j]).reshape(t, -1)]
            w_out = c_w_out[j]
        x2d = _block_tail(x2d, mixes, w_out, norm_g[layer, 1], norm_g[layer, 2], norm_g[layer, 3],
                          ffn_w1[layer], ffn_w3[layer], ffn_w2[layer])
    return x2d.reshape(bsz, seq, d)
```

```python
import functools
import math

import jax
import jax.numpy as jnp
from jax import lax
from jax.experimental import pallas as pl
from jax.experimental.pallas import tpu as pltpu

F32 = jnp.float32
BF16 = jnp.bfloat16
EPS = 1e-6

V7X_VMEM_LIMIT_BYTES = 56 * 1024 * 1024
SUBLANES = 8
LANES = 128
MXU_TILE = 256

M_HEADS = 4
M_CONV = 4
QKV_BLOCK = 4
S_GROUP = 16
S_STATE = 64
H_EXPAND = 128

ROW_TILE_PROJ = 512
ROW_TILE_TAIL = 1024
MLSTM_CHUNK = 256
S5_TILE = 256
S5_TILES_PER_PASS = 8
HGRN_TILE = 256
HGRN_CHUNK_UNROLL = 2
TAIL_ROW_GROUPS = 4
HGRN_CHUNK = 64


def _dot(a, b):
    return jnp.dot(a, b, preferred_element_type=F32)


def _dot_nt(a, b):
    return lax.dot_general(a, b, (((1,), (1,)), ((), ())), preferred_element_type=F32)


def _dot_tn(a, b):
    return lax.dot_general(a, b, (((0,), (0,)), ((), ())), preferred_element_type=F32)


def _sigmoid(x):
    return 1.0 / (1.0 + jnp.exp(-x))


def _silu(x):
    return x * _sigmoid(x)


def _log_sigmoid(x):
    return jnp.minimum(x, 0.0) - jnp.log(1.0 + jnp.exp(-jnp.abs(x)))


def _gelu_tanh(x):
    return 0.5 * x * (1.0 + jnp.tanh(math.sqrt(2.0 / math.pi) * (x + 0.044715 * (x * x * x))))


def _rms(x, g):
    return x * lax.rsqrt(jnp.mean(x * x, axis=-1, keepdims=True) + EPS) * g


def _prefix_rows(x, op, identity):
    rows, n = x.shape
    row = lax.broadcasted_iota(jnp.int32, (SUBLANES, n), 0)
    outs = []
    carry = None
    for r in range(rows // SUBLANES):
        t = x[r * SUBLANES:(r + 1) * SUBLANES, :]
        for k in (1, 2, 4):
            t = op(t, jnp.where(row >= k, pltpu.roll(t, k, axis=0), identity))
        if carry is not None:
            t = op(t, carry)
        carry = t[SUBLANES - 1:SUBLANES, :]
        outs.append(t)
    return jnp.concatenate(outs, axis=0)


def _cumsum_rows(x):
    return _prefix_rows(x, jnp.add, 0.0)


def _cummax_rows(x):
    return _prefix_rows(x, jnp.maximum, -jnp.inf)


def _params(n_axes):
    return pltpu.CompilerParams(dimension_semantics=("arbitrary",) * n_axes,
                                vmem_limit_bytes=V7X_VMEM_LIMIT_BYTES)


def _const_spec(shape):
    nd = len(shape)
    return pl.BlockSpec(shape, lambda *_: (0,) * nd, pipeline_mode=pl.Buffered(1))


def _rms_matmul_kernel(x_ref, g_ref, w_ref, o_ref):
    h = _rms(x_ref[...], g_ref[...])
    o_ref[...] = _dot(h.astype(BF16), w_ref[...]).astype(o_ref.dtype)


def _rms_matmul(x2d, g, w, tm):
    t, d = x2d.shape
    n = w.shape[1]
    return pl.pallas_call(
        _rms_matmul_kernel,
        grid=(t // tm,),
        in_specs=[pl.BlockSpec((tm, d), lambda i: (i, 0)), _const_spec((1, d)), _const_spec((d, n))],
        out_specs=pl.BlockSpec((tm, n), lambda i: (i, 0)),
        out_shape=jax.ShapeDtypeStruct((t, n), F32),
        compiler_params=_params(1),
        name="rms_proj",
    )(x2d, g.reshape(1, d), w)


def _mlstm_kernel(x_ref, ng_ref, win_ref, convw_ref, convb_ref, wq_ref, wk_ref, wv_ref, wif_ref, bif_ref, gain_ref,
                  skip_ref, o_ref, xbuf, c_s, m_s, *, heads):
    n_rows, chunk, width = o_ref.shape
    hd = width // heads
    taps = convw_ref.shape[0]
    rows = range(n_rows)

    @pl.when(pl.program_id(0) == 0)
    def _init():
        xbuf[:, 0:SUBLANES, :] = jnp.zeros((n_rows, SUBLANES, width), F32)
        c_s[...] = jnp.zeros_like(c_s)
        m_s[...] = jnp.zeros_like(m_s)

    proj = [_dot(_rms(x_ref[r], ng_ref[...]).astype(BF16), win_ref[...]) for r in rows]
    xm = [p[:, 0:width] for p in proj]
    zm = [p[:, width:2 * width] for p in proj]

    xc = []
    for r in rows:
        xbuf[r, SUBLANES:SUBLANES + chunk, :] = xm[r]
        conv = convb_ref[...] + convw_ref[taps - 1:taps, :] * xm[r]
        for k in range(taps - 1):
            conv = conv + convw_ref[k:k + 1, :] * xbuf[r, pl.ds(SUBLANES - (taps - 1) + k, chunk), :]
        xbuf[r, 0:SUBLANES, :] = xm[r][chunk - SUBLANES:chunk, :]
        xc.append(_silu(conv))

    q_b, k, k_b, v_b, gates = [], [], [], [], []
    for r in rows:
        xc_b = xc[r].astype(BF16)
        q_b.append(_dot(xc_b, wq_ref[...]).astype(BF16))
        k.append(_dot(xc_b, wk_ref[...]) * (hd ** -0.5))
        k_b.append(k[r].astype(BF16))
        v_b.append(_dot(xm[r].astype(BF16), wv_ref[...]).astype(BF16))
        gates.append(_dot(q_b[r], wif_ref[0:width, :]) + _dot(k_b[r], wif_ref[width:2 * width, :])
                     + _dot(v_b[r], wif_ref[2 * width:3 * width, :]) + bif_ref[...])

    m_prev, row_term, e_neg_m, k_scale, s_old, s_new, g_rows = [], [], [], [], [], [], []
    for r in rows:
        log_i = gates[r][:, 0:LANES]
        b = _cumsum_rows(_log_sigmoid(gates[r][:, LANES:2 * LANES]))
        g = log_i - b
        g_max = _cummax_rows(g)
        m_prev.append(m_s[r, 0:1, :])
        m_t = jnp.maximum(b + m_prev[r], b + g_max)
        row_term.append(b - m_t)
        e_neg_m.append(jnp.exp(-m_t))
        b_last = b[chunk - 1:chunk, :]
        a_max = b_last + g_max[chunk - 1:chunk, :]
        k_scale.append(jnp.exp(g - g_max[chunk - 1:chunk, :]))
        m_new = jnp.maximum(b_last + m_prev[r], a_max)
        s_old.append(jnp.exp(b_last + m_prev[r] - m_new))
        s_new.append(jnp.exp(a_max - m_new))
        m_s[r] = jnp.broadcast_to(m_new, m_s.shape[1:])
        g_rows.append(g.T)

    causal = _causal_mask(chunk)
    ones_b = jnp.ones((chunk, hd), BF16)

    for h in range(heads):
        sl = slice(h * hd, (h + 1) * hd)
        for r in rows:
            qh_b, kh_b = q_b[r][:, sl], k_b[r][:, sl]
            v_ones = jnp.concatenate([v_b[r][:, sl], ones_b], axis=1)
            row_b = jnp.broadcast_to(row_term[r][:, h:h + 1], (chunk, hd))
            row_w = jnp.concatenate([row_b] * (chunk // hd), axis=1)
            w_intra = jnp.exp(jnp.where(causal, row_w + g_rows[r][h:h + 1, :], -jnp.inf))
            s_inter = jnp.exp(row_b + m_prev[r][:, h:h + 1])
            scores = (_dot_nt(qh_b, kh_b) * w_intra).astype(BF16)
            c_prev = c_s[r, h]
            tot = (_dot(scores, v_ones)
                   + jnp.concatenate([s_inter, s_inter], axis=1) * _dot(qh_b, c_prev.astype(BF16)))
            den_floor = jnp.broadcast_to(e_neg_m[r][:, h:h + 1], (chunk, hd))
            hh = tot[:, 0:hd] / jnp.maximum(jnp.abs(tot[:, hd:2 * hd]), den_floor)

            kw = (k[r][:, sl] * jnp.broadcast_to(k_scale[r][:, h:h + 1], (chunk, hd))).astype(BF16)
            c_s[r, h] = s_old[r][:, h:h + 1] * c_prev + s_new[r][:, h:h + 1] * _dot_tn(kw, v_ones)

            mu = jnp.mean(hh, axis=-1, keepdims=True)
            dev = hh - mu
            var = jnp.mean(dev * dev, axis=-1, keepdims=True)
            hn = dev * lax.rsqrt(var + EPS) * gain_ref[:, sl]
            o_ref[r, :, sl] = ((hn + skip_ref[:, sl] * xc[r][:, sl]) * _silu(zm[r][:, sl])).astype(o_ref.dtype)


def _mlstm(x, norm_gain, w_in, conv_w, conv_b, wq, wk, wv, w_if, b_if, gain, skip, heads):
    b, l, d = x.shape
    width = conv_w.shape[1]
    hd = width // heads
    chunk = MLSTM_CHUNK
    def block_diag(w):
        nb, d, e = w.shape
        tiled = jnp.tile(w.reshape(nb * d, e), (1, nb))
        same_block = (jnp.arange(nb * d) // d)[:, None] == (jnp.arange(nb * e) // e)[None, :]
        return jnp.where(same_block, tiled, 0.0).astype(BF16)

    wif_pad = (jnp.zeros((w_if.shape[0], 2 * LANES), F32).at[:, :heads].set(w_if[:, :heads])
               .at[:, LANES:LANES + heads].set(w_if[:, heads:])).astype(BF16)
    bif_pad = (jnp.zeros((1, 2 * LANES), F32).at[0, :heads].set(b_if[:heads])
               .at[0, LANES:LANES + heads].set(b_if[heads:]))
    kern = functools.partial(_mlstm_kernel, heads=heads)
    return pl.pallas_call(
        kern,
        grid=(l // chunk,),
        in_specs=[pl.BlockSpec((b, chunk, d), lambda i: (0, i, 0)), _const_spec((1, d)), _const_spec(w_in.shape),
                  _const_spec(conv_w.shape), _const_spec((1, width)),
                  _const_spec((width, width)), _const_spec((width, width)), _const_spec((width, width)),
                  _const_spec((3 * width, 2 * LANES)), _const_spec((1, 2 * LANES)),
                  _const_spec((1, width)), _const_spec((1, width))],
        out_specs=pl.BlockSpec((b, chunk, width), lambda i: (0, i, 0)),
        out_shape=jax.ShapeDtypeStruct((b, l, width), BF16),
        scratch_shapes=[pltpu.VMEM((b, chunk + SUBLANES, width), F32),
                        pltpu.VMEM((b, heads, hd, 2 * hd), F32),
                        pltpu.VMEM((b, SUBLANES, LANES), F32)],
        compiler_params=_params(1),
        name="mlstm",
    )(x, norm_gain.reshape(1, d), w_in.astype(BF16), conv_w, conv_b.reshape(1, width),
      block_diag(wq), block_diag(wk), block_diag(wv),
      wif_pad, bif_pad, gain.reshape(1, width), skip.reshape(1, width))


def _s5_kernel(x_ref, ng_ref, win_ref, pm_ref, pmt_ref, wb_ref, cst_ref, wc_ref, d_ref, wglu_ref, bglu_ref,
               o_ref, xs, carry):
    n_rows, tm, width = o_ref.shape
    seg = tm // SUBLANES
    n_tiles = wb_ref.shape[0]
    kt = wb_ref.shape[1]
    tiles_per_k = n_tiles // (width // kt)
    xw = 2 * LANES
    rows = range(n_rows)

    @pl.when(pl.program_id(0) == 0)
    def _init():
        carry[...] = jnp.zeros_like(carry)

    u = []
    for r in rows:
        h_b = _rms(x_ref[r], ng_ref[...]).astype(BF16)
        u.append(_dot(_dot(pm_ref[...], h_b).astype(BF16), win_ref[...]))
    for r in rows:
        u_b = u[r].astype(BF16)
        for j in range(n_tiles):
            kk = j // tiles_per_k
            xs[r, :, j * xw:(j + 1) * xw] = _dot(u_b[:, kk * kt:(kk + 1) * kt], wb_ref[j])

    row_id = lax.broadcasted_iota(jnp.int32, (SUBLANES, LANES), 0)

    def recur(r, t, state, tiles, store):
        r8 = t * SUBLANES
        new = []
        for idx, j in enumerate(tiles):
            ar, ai = cst_ref[j, 0], cst_ref[j, 1]
            xr, xi = state[2 * idx], state[2 * idx + 1]
            br = xs[r, pl.ds(r8, SUBLANES), j * xw:j * xw + LANES]
            bi = xs[r, pl.ds(r8, SUBLANES), j * xw + LANES:(j + 1) * xw]
            xr, xi = (ar * xr - ai * xi) + br, (ar * xi + ai * xr) + bi
            if store:
                xs[r, pl.ds(r8, SUBLANES), j * xw:j * xw + LANES] = xr
                xs[r, pl.ds(r8, SUBLANES), j * xw + LANES:(j + 1) * xw] = xi
            new += [xr, xi]
        return new

    n_out = wc_ref.shape[0]
    kc = wc_ref.shape[1]
    for r in rows:
        for j0 in range(0, n_tiles, S5_TILES_PER_PASS):
            tiles = tuple(range(j0, min(j0 + S5_TILES_PER_PASS, n_tiles)))
            state = [jnp.zeros((SUBLANES, LANES), F32) for _ in range(2 * len(tiles))]
            for t in range(seg):
                state = recur(r, t, state, tiles, False)

            starts = []
            for idx, j in enumerate(tiles):
                er, ei = state[2 * idx], state[2 * idx + 1]
                for lvl, k in enumerate((1, 2, 4)):
                    mr, mi = cst_ref[j, 2 + 2 * lvl], cst_ref[j, 3 + 2 * lvl]
                    sr, si = pltpu.roll(er, k, axis=0), pltpu.roll(ei, k, axis=0)
                    er, ei = er + (mr * sr - mi * si), ei + (mr * si + mi * sr)
                pr, pi = cst_ref[j, 8], cst_ref[j, 9]
                cr, ci = carry[r, j, 0], carry[r, j, 1]
                er, ei = er + (pr * cr - pi * ci), ei + (pr * ci + pi * cr)
                carry[r, j, 0] = jnp.broadcast_to(er[SUBLANES - 1:SUBLANES, :], (SUBLANES, LANES))
                carry[r, j, 1] = jnp.broadcast_to(ei[SUBLANES - 1:SUBLANES, :], (SUBLANES, LANES))
                starts.append(jnp.where(row_id == 0, cr, pltpu.roll(er, 1, axis=0)))
                starts.append(jnp.where(row_id == 0, ci, pltpu.roll(ei, 1, axis=0)))
            state = starts
            for t in range(seg):
                state = recur(r, t, state, tiles, True)

        y = jnp.concatenate([_dot(xs[r, :, h * kc:(h + 1) * kc].astype(BF16), wc_ref[h]) for h in range(n_out)],
                            axis=1)
        y = _gelu_tanh(y + d_ref[...] * u[r])
        gate = _sigmoid(_dot(y.astype(BF16), wglu_ref[...]) + bglu_ref[...])
        o_ref[r] = _dot(pmt_ref[...], (y * gate).astype(BF16)).astype(o_ref.dtype)


def _s5_weights(a_re, a_im, log_dt, b_re, b_im, c_re, c_im, seg):
    groups, n_state, p = b_re.shape
    width = groups * p
    gpt = LANES // n_state
    n_tiles = groups // gpt
    dt = jnp.exp(log_dt)[:, None]
    mag = jnp.exp(a_re * dt)
    ab_re = mag * jnp.cos(a_im * dt)
    ab_im = mag * jnp.sin(a_im * dt)
    inv = 1.0 / (a_re * a_re + a_im * a_im)
    g_re = ((ab_re - 1.0) * a_re + ab_im * a_im) * inv
    g_im = (ab_im * a_re - (ab_re - 1.0) * a_im) * inv
    bb_re = g_re[..., None] * b_re - g_im[..., None] * b_im
    bb_im = g_re[..., None] * b_im + g_im[..., None] * b_re
    kt = min(MXU_TILE, width)
    n_k = width // kt
    tiles_per_k = n_tiles // n_k
    xw = 2 * LANES

    bb = jnp.stack([bb_re, bb_im], axis=0)
    r_in = jnp.transpose(bb, (1, 3, 0, 2))
    r_in = jnp.broadcast_to(r_in[:, :, :, None, :], (groups, p, 2, gpt, n_state)).reshape(width, xw)
    row_g = jnp.arange(width) // p
    col_gg = (jnp.arange(xw) % LANES) // n_state
    dense_in = jnp.where((row_g % gpt)[:, None] == col_gg[None, :], r_in, 0.0).reshape(n_k, kt, xw)
    tile_id = jnp.arange(n_tiles)
    row_tile = (row_g // gpt).reshape(n_k, kt)[tile_id // tiles_per_k]
    wb = jnp.where(row_tile[:, :, None] == tile_id[:, None, None], dense_in[tile_id // tiles_per_k], 0.0)

    cc = jnp.stack([c_re, -c_im], axis=0)
    r_out = jnp.transpose(cc, (1, 0, 3, 2)).reshape(n_tiles, gpt, 2, n_state, p)
    r_out = jnp.transpose(r_out, (0, 2, 1, 3, 4)).reshape(n_tiles * xw, p)
    r_out = jnp.tile(r_out, (1, kt // p))
    out_row_g = (jnp.arange(n_tiles * xw) // xw) * gpt + (jnp.arange(n_tiles * xw) % LANES) // n_state
    out_col_g = jnp.arange(kt) // p
    wc = jnp.where((out_row_g % (kt // p))[:, None] == out_col_g[None, :], r_out, 0.0)
    wc = wc.reshape(n_k, tiles_per_k * xw, kt)

    a1 = (ab_re.reshape(n_tiles, LANES), ab_im.reshape(n_tiles, LANES))

    def cmul(x, y):
        return x[0] * y[0] - x[1] * y[1], x[0] * y[1] + x[1] * y[0]

    a_seg = a1
    assert seg & (seg - 1) == 0, "segment length must be a power of two"
    for _ in range(seg.bit_length() - 1):
        a_seg = cmul(a_seg, a_seg)
    pows = [a_seg]
    for _ in range(SUBLANES - 1):
        pows.append(cmul(pows[-1], a_seg))
    row = jnp.arange(SUBLANES)[None, :, None]
    planes = [jnp.broadcast_to(a1[c][:, None, :], (n_tiles, SUBLANES, LANES)) for c in range(2)]
    for k in (1, 2, 4):
        for c in range(2):
            planes.append(jnp.where(row >= k, pows[k - 1][c][:, None, :], 0.0))
    for c in range(2):
        planes.append(jnp.stack([pw[c] for pw in pows], axis=1))
    cst = jnp.stack(planes, axis=1).astype(F32)
    return wb.astype(BF16), cst, wc.astype(BF16)


def _s5(x, norm_gain, w_in, a_re, a_im, log_dt, b_re, b_im, c_re, c_im, d_skip, w_glu, b_glu):
    b, l, d = x.shape
    width = d_skip.shape[0]
    tm = S5_TILE
    seg = tm // SUBLANES
    wb, cst, wc = _s5_weights(a_re, a_im, log_dt, b_re, b_im, c_re, c_im, seg)
    n_tiles = wb.shape[0]
    rp = jnp.arange(tm)
    pm = ((rp % SUBLANES) * seg + rp // SUBLANES)[:, None] == rp[None, :]
    return pl.pallas_call(
        _s5_kernel,
        grid=(l // tm,),
        in_specs=[pl.BlockSpec((b, tm, d), lambda i: (0, i, 0)), _const_spec((1, d)), _const_spec(w_in.shape),
                  _const_spec((tm, tm)), _const_spec((tm, tm)),
                  _const_spec(wb.shape), _const_spec(cst.shape), _const_spec(wc.shape),
                  _const_spec((1, width)), _const_spec((width, width)), _const_spec((1, width))],
        out_specs=pl.BlockSpec((b, tm, width), lambda i: (0, i, 0)),
        out_shape=jax.ShapeDtypeStruct((b, l, width), BF16),
        scratch_shapes=[pltpu.VMEM((b, tm, n_tiles * 2 * LANES), F32),
                        pltpu.VMEM((b, n_tiles, 2, SUBLANES, LANES), F32)],
        compiler_params=_params(1),
        name="s5",
    )(x, norm_gain.reshape(1, d), w_in.astype(BF16), pm.astype(BF16), pm.T.astype(BF16), wb, cst, wc,
      d_skip.reshape(1, width), w_glu.astype(BF16),
      b_glu.reshape(1, width))


def _hgrn2_chunk(p_ref, r0, lb, gain, st, causal, o_ref, *, heads, chunk):
    n_rows, _, width = o_ref.shape
    e = width // heads
    rows = pl.ds(r0, chunk)
    units = [(h, r) for h in range(heads) for r in range(n_rows)]

    q_inter, k_state, q_intra, v_b, gate, decay = [], [], [], [], [], []
    for h, r in units:
        sl = slice(h * e, (h + 1) * e)
        q = _silu(p_ref[r, rows, h * e:(h + 1) * e])
        f = p_ref[r, rows, width + h * e:width + (h + 1) * e]
        g = p_ref[r, rows, 3 * width + h * e:3 * width + (h + 1) * e]
        lb_h = lb[:, sl]
        fg = lb_h + (1.0 - lb_h) * _sigmoid(f)
        b = _cumsum_rows(jnp.log(fg))
        b_last = b[chunk - 1:chunk, :]
        q_inter.append((q * jnp.exp(b)).astype(BF16))
        k_state.append(((1.0 - fg) * jnp.exp(b_last - b)).astype(BF16))
        q_intra.append((q * jnp.exp(b - b_last)).astype(BF16))
        v_b.append(p_ref[r, rows, 2 * width + h * e:2 * width + (h + 1) * e].astype(BF16))
        gate.append(gain[:, sl] * _silu(g))
        decay.append(jnp.exp(b_last))

    scores = [_dot_nt(q_intra[u], k_state[u]) for u in range(len(units))]
    scores = [jnp.where(causal, sc, 0.0).astype(BF16) for sc in scores]
    outs = []
    for u, (h, r) in enumerate(units):
        s_t = st[r, h]
        outs.append(_dot(scores[u], v_b[u]) + _dot_nt(q_inter[u], s_t.astype(BF16)))
        st[r, h] = s_t * decay[u] + _dot_tn(v_b[u], k_state[u])
    for u, (h, r) in enumerate(units):
        o = outs[u]
        o = o * lax.rsqrt(jnp.mean(o * o, axis=-1, keepdims=True) + EPS)
        o_ref[r, rows, h * e:(h + 1) * e] = (o * gate[u]).astype(o_ref.dtype)


def _causal_mask(chunk):
    ti = lax.broadcasted_iota(jnp.int32, (chunk, chunk), 0)
    si = lax.broadcasted_iota(jnp.int32, (chunk, chunk), 1)
    return si <= ti


def _tail_math(xs, mix_groups, wo_ref, g1_ref, g2_ref, g3_ref, w1_ref, w3_ref, w2_ref):
    mps = []
    for mix_refs in mix_groups:
        mp = None
        row0 = 0
        for m_ref in mix_refs:
            part = _dot(m_ref[...], wo_ref[row0:row0 + m_ref.shape[1], :])
            mp = part if mp is None else mp + part
            row0 += m_ref.shape[1]
        mps.append(mp)
    x1s = [x + _rms(mp, g1_ref[...]) for x, mp in zip(xs, mps)]
    hs = [_rms(x1, g2_ref[...]).astype(BF16) for x1 in x1s]
    ys = []
    for h in hs:
        z = (_silu(_dot(h, w1_ref[...])) * _dot(h, w3_ref[...])).astype(BF16)
        ys.append(_dot(z, w2_ref[...]))
    return [x1 + _rms(y, g3_ref[...]) for x1, y in zip(x1s, ys)]


def _hgrn2_kernel(p_ref, lb_ref, gain_ref, o_ref, st, *, heads, chunk):
    tm = o_ref.shape[1]

    @pl.when(pl.program_id(0) == 0)
    def _init():
        st[...] = jnp.zeros_like(st)

    lb = lb_ref[...]
    gain = gain_ref[...]
    causal = _causal_mask(chunk)

    def body(c, _):
        _hgrn2_chunk(p_ref, pl.multiple_of(c * chunk, chunk), lb, gain, st, causal, o_ref, heads=heads, chunk=chunk)
        return 0

    lax.fori_loop(0, tm // chunk, body, 0, unroll=HGRN_CHUNK_UNROLL)


def _hgrn2(proj, lb, gain):
    b, l, w4 = proj.shape
    width = w4 // 4
    heads = width // H_EXPAND
    tm = HGRN_TILE
    kern = functools.partial(_hgrn2_kernel, heads=heads, chunk=HGRN_CHUNK)
    return pl.pallas_call(
        kern,
        grid=(l // tm,),
        in_specs=[pl.BlockSpec((b, tm, w4), lambda i: (0, i, 0)),
                  _const_spec((1, width)), _const_spec((1, width))],
        out_specs=pl.BlockSpec((b, tm, width), lambda i: (0, i, 0)),
        out_shape=jax.ShapeDtypeStruct((b, l, width), BF16),
        scratch_shapes=[pltpu.VMEM((b, heads, H_EXPAND, H_EXPAND), F32)],
        compiler_params=_params(1),
        name="hgrn2",
    )(proj, lb.reshape(1, width), gain.reshape(1, width))


def _tail_kernel(*refs, n_mix):
    x_ref = refs[0]
    mix_refs = refs[1:1 + n_mix]
    wo_ref, g1_ref, g2_ref, g3_ref, w1_ref, w3_ref, w2_ref, o_ref = refs[1 + n_mix:]
    rows = o_ref.shape[0] // TAIL_ROW_GROUPS
    groups = [slice(r * rows, (r + 1) * rows) for r in range(TAIL_ROW_GROUPS)]
    outs = _tail_math([x_ref[sl, :] for sl in groups], [[m.at[sl, :] for m in mix_refs] for sl in groups],
                      wo_ref, g1_ref, g2_ref, g3_ref, w1_ref, w3_ref, w2_ref)
    for sl, out in zip(groups, outs):
        o_ref[sl, :] = out


def _block_tail(x2d, mixes, w_out, g1, g2, g3, w1, w3, w2):
    t, d = x2d.shape
    f = w1.shape[1]
    tm = ROW_TILE_TAIL
    n_mix = len(mixes)
    row = lambda i: (i, 0)
    in_specs = ([pl.BlockSpec((tm, d), row)]
                + [pl.BlockSpec((tm, m.shape[1]), row) for m in mixes]
                + [_const_spec(w_out.shape)]
                + [_const_spec((1, d))] * 3
                + [_const_spec((d, f)), _const_spec((d, f)), _const_spec((f, d))])
    return pl.pallas_call(
        functools.partial(_tail_kernel, n_mix=n_mix),
        grid=(t // tm,),
        in_specs=in_specs,
        out_specs=pl.BlockSpec((tm, d), row),
        out_shape=jax.ShapeDtypeStruct((t, d), F32),
        compiler_params=_params(1),
        name="outproj_swiglu",
    )(x2d, *mixes, w_out.astype(BF16), g1.reshape(1, d), g2.reshape(1, d), g3.reshape(1, d),
      w1.astype(BF16), w3.astype(BF16), w2.astype(BF16))


def kernel(x, norm_g, ab_w_in, ab_conv_w, ab_conv_b, ab_wq, ab_wk, ab_wv, ab_w_if, ab_b_if, ab_mh_gain, ab_skip,
           ab_a_re, ab_a_im, ab_log_dt, ab_b_re, ab_b_im, ab_c_re, ab_c_im, ab_d, ab_w_glu, ab_b_glu, ab_w_out,
           c_w_in, c_lb_raw, c_g_gain, c_w_out, ffn_w1, ffn_w3, ffn_w2):
    bsz, seq, d = x.shape
    t = bsz * seq
    depth = norm_g.shape[0]
    x2d = x.reshape(t, d)
    lbs = jnp.cumsum(jax.nn.softmax(c_lb_raw.astype(F32), axis=0), axis=0)
    for layer in range(depth):
        j = layer // 2
        if layer % 2 == 0:
            m_width = ab_conv_w.shape[2]
            s_width = ab_d.shape[1]
            x3d = x2d.reshape(bsz, seq, d)
            out_m = _mlstm(x3d, norm_g[layer, 0], ab_w_in[j][:, :2 * m_width], ab_conv_w[j], ab_conv_b[j], ab_wq[j],
                           ab_wk[j], ab_wv[j], ab_w_if[j], ab_b_if[j], ab_mh_gain[j], ab_skip[j], M_HEADS)
            out_s = _s5(x3d, norm_g[layer, 0], ab_w_in[j][:, 2 * m_width:], ab_a_re[j], ab_a_im[j], ab_log_dt[j],
                        ab_b_re[j], ab_b_im[j], ab_c_re[j], ab_c_im[j], ab_d[j], ab_w_glu[j], ab_b_glu[j])
            mixes = [out_m.reshape(t, m_width), out_s.reshape(t, s_width)]
            w_out = ab_w_out[j]
        else:
            proj = _rms_matmul(x2d, norm_g[layer, 0], c_w_in[j].astype(BF16), ROW_TILE_PROJ).reshape(bsz, seq, -1)
            lb = lbs[layer] - lbs[0]
            mixes = [_hgrn2(proj, lb, c_g_gain[j]).reshape(t, -1)]
            w_out = c_w_out[j]
        x2d = _block_tail(x2d, mixes, w_out, norm_g[layer, 1], norm_g[layer, 2], norm_g[layer, 3],
                          ffn_w1[layer], ffn_w3[layer], ffn_w2[layer])
    return x2d.reshape(bsz, seq, d)
```

```python
import functools
import math

import jax
import jax.numpy as jnp
from jax import lax
from jax.experimental import pallas as pl
from jax.experimental.pallas import tpu as pltpu

F32 = jnp.float32
BF16 = jnp.bfloat16
EPS = 1e-6

V7X_VMEM_LIMIT_BYTES = 56 * 1024 * 1024
SUBLANES = 8
LANES = 128
MXU_TILE = 256

M_HEADS = 4
M_CONV = 4
QKV_BLOCK = 4
S_GROUP = 16
S_STATE = 64
H_EXPAND = 128

ROW_TILE_PROJ = 512
ROW_TILE_TAIL = 1024
MLSTM_CHUNK = 256
S5_TILE = 256
S5_TILES_PER_PASS = 8
HGRN_TILE = 256
HGRN_CHUNK_UNROLL = 2
TAIL_ROW_GROUPS = 4
HGRN_CHUNK = 64


def _dot(a, b):
    return jnp.dot(a, b, preferred_element_type=F32)


def _dot_nt(a, b):
    return lax.dot_general(a, b, (((1,), (1,)), ((), ())), preferred_element_type=F32)


def _dot_tn(a, b):
    return lax.dot_general(a, b, (((0,), (0,)), ((), ())), preferred_element_type=F32)


def _sigmoid(x):
    return 1.0 / (1.0 + jnp.exp(-x))


def _silu(x):
    return x * _sigmoid(x)


def _log_sigmoid(x):
    return jnp.minimum(x, 0.0) - jnp.log(1.0 + jnp.exp(-jnp.abs(x)))


def _gelu_tanh(x):
    return 0.5 * x * (1.0 + jnp.tanh(math.sqrt(2.0 / math.pi) * (x + 0.044715 * (x * x * x))))


def _rms(x, g):
    return x * lax.rsqrt(jnp.mean(x * x, axis=-1, keepdims=True) + EPS) * g


def _prefix_rows(x, op, identity):
    rows, n = x.shape
    row = lax.broadcasted_iota(jnp.int32, (SUBLANES, n), 0)
    outs = []
    carry = None
    for r in range(rows // SUBLANES):
        t = x[r * SUBLANES:(r + 1) * SUBLANES, :]
        for k in (1, 2, 4):
            t = op(t, jnp.where(row >= k, pltpu.roll(t, k, axis=0), identity))
        if carry is not None:
            t = op(t, carry)
        carry = t[SUBLANES - 1:SUBLANES, :]
        outs.append(t)
    return jnp.concatenate(outs, axis=0)


def _cumsum_rows(x):
    return _prefix_rows(x, jnp.add, 0.0)


def _cummax_rows(x):
    return _prefix_rows(x, jnp.maximum, -jnp.inf)


def _params(n_axes):
    return pltpu.CompilerParams(dimension_semantics=("arbitrary",) * n_axes,
                                vmem_limit_bytes=V7X_VMEM_LIMIT_BYTES)


def _const_spec(shape):
    nd = len(shape)
    return pl.BlockSpec(shape, lambda *_: (0,) * nd, pipeline_mode=pl.Buffered(1))


def _hgrn2_proj_kernel(x_ref, g_ref, w_ref, gain_ref, o_ref):
    width = gain_ref.shape[1]
    h = _rms(x_ref[...], g_ref[...]).astype(BF16)
    o_ref[:, 0:width] = _silu(_dot(h, w_ref[:, 0:width]))
    o_ref[:, width:3 * width] = _dot(h, w_ref[:, width:3 * width])
    o_ref[:, 3 * width:4 * width] = gain_ref[...] * _silu(_dot(h, w_ref[:, 3 * width:4 * width]))


def _hgrn2_proj(x2d, g, w, gain, tm):
    t, d = x2d.shape
    n = w.shape[1]
    width = n // 4
    return pl.pallas_call(
        _hgrn2_proj_kernel,
        grid=(t // tm,),
        in_specs=[pl.BlockSpec((tm, d), lambda i: (i, 0)), _const_spec((1, d)), _const_spec((d, n)),
                  _const_spec((1, width))],
        out_specs=pl.BlockSpec((tm, n), lambda i: (i, 0)),
        out_shape=jax.ShapeDtypeStruct((t, n), F32),
        compiler_params=_params(1),
        name="hgrn2_proj",
    )(x2d, g.reshape(1, d), w, gain.reshape(1, width))


def _mlstm_kernel(x_ref, ng_ref, win_ref, convw_ref, convb_ref, wq_ref, wk_ref, wv_ref, wif_ref, bif_ref, gain_ref,
                  skip_ref, o_ref, xbuf, c_s, m_s, *, heads):
    n_rows, chunk, width = o_ref.shape
    hd = width // heads
    taps = convw_ref.shape[0]
    rows = range(n_rows)

    @pl.when(pl.program_id(0) == 0)
    def _init():
        xbuf[:, 0:SUBLANES, :] = jnp.zeros((n_rows, SUBLANES, width), F32)
        c_s[...] = jnp.zeros_like(c_s)
        m_s[...] = jnp.zeros_like(m_s)

    proj = [_dot(_rms(x_ref[r], ng_ref[...]).astype(BF16), win_ref[...]) for r in rows]
    xm = [p[:, 0:width] for p in proj]
    zm = [p[:, width:2 * width] for p in proj]

    xc = []
    for r in rows:
        xbuf[r, SUBLANES:SUBLANES + chunk, :] = xm[r]
        conv = convb_ref[...] + convw_ref[taps - 1:taps, :] * xm[r]
        for k in range(taps - 1):
            conv = conv + convw_ref[k:k + 1, :] * xbuf[r, pl.ds(SUBLANES - (taps - 1) + k, chunk), :]
        xbuf[r, 0:SUBLANES, :] = xm[r][chunk - SUBLANES:chunk, :]
        xc.append(_silu(conv))

    q_b, k, k_b, v_b, gates = [], [], [], [], []
    for r in rows:
        xc_b = xc[r].astype(BF16)
        q_b.append(_dot(xc_b, wq_ref[...]).astype(BF16))
        k.append(_dot(xc_b, wk_ref[...]) * (hd ** -0.5))
        k_b.append(k[r].astype(BF16))
        v_b.append(_dot(xm[r].astype(BF16), wv_ref[...]).astype(BF16))
        gates.append(_dot(q_b[r], wif_ref[0:width, :]) + _dot(k_b[r], wif_ref[width:2 * width, :])
                     + _dot(v_b[r], wif_ref[2 * width:3 * width, :]) + bif_ref[...])

    m_prev, row_term, e_neg_m, k_scale, s_old, s_new, g_rows = [], [], [], [], [], [], []
    for r in rows:
        log_i = gates[r][:, 0:LANES]
        b = _cumsum_rows(_log_sigmoid(gates[r][:, LANES:2 * LANES]))
        g = log_i - b
        g_max = _cummax_rows(g)
        m_prev.append(m_s[r, 0:1, :])
        m_t = jnp.maximum(b + m_prev[r], b + g_max)
        row_term.append(b - m_t)
        e_neg_m.append(jnp.exp(-m_t))
        b_last = b[chunk - 1:chunk, :]
        a_max = b_last + g_max[chunk - 1:chunk, :]
        k_scale.append(jnp.exp(g - g_max[chunk - 1:chunk, :]))
        m_new = jnp.maximum(b_last + m_prev[r], a_max)
        s_old.append(jnp.exp(b_last + m_prev[r] - m_new))
        s_new.append(jnp.exp(a_max - m_new))
        m_s[r] = jnp.broadcast_to(m_new, m_s.shape[1:])
        g_rows.append(g.T)

    causal = _causal_mask(chunk)
    ones_b = jnp.ones((chunk, hd), BF16)

    units = [(h, r) for h in range(heads) for r in rows]
    cols = [slice(h * hd, (h + 1) * hd) for h, _ in units]

    v_ones = [jnp.concatenate([v_b[r][:, sl], ones_b], axis=1) for (h, r), sl in zip(units, cols)]
    qk = [_dot_nt(q_b[r][:, sl], k_b[r][:, sl]) for (h, r), sl in zip(units, cols)]
    inter = [_dot(q_b[r][:, sl], c_s[r, h].astype(BF16)) for (h, r), sl in zip(units, cols)]

    scores, s_inter = [], []
    for u, (h, r) in enumerate(units):
        row_b = jnp.broadcast_to(row_term[r][:, h:h + 1], (chunk, hd))
        row_w = jnp.concatenate([row_b] * (chunk // hd), axis=1)
        w_intra = jnp.exp(jnp.where(causal, row_w + g_rows[r][h:h + 1, :], -jnp.inf))
        s_inter.append(jnp.exp(row_b + m_prev[r][:, h:h + 1]))
        scores.append((qk[u] * w_intra).astype(BF16))

    tot = [_dot(scores[u], v_ones[u]) + jnp.concatenate([s_inter[u], s_inter[u]], axis=1) * inter[u]
           for u in range(len(units))]

    for u, ((h, r), sl) in enumerate(zip(units, cols)):
        kw = (k[r][:, sl] * jnp.broadcast_to(k_scale[r][:, h:h + 1], (chunk, hd))).astype(BF16)
        c_s[r, h] = s_old[r][:, h:h + 1] * c_s[r, h] + s_new[r][:, h:h + 1] * _dot_tn(kw, v_ones[u])

    for u, ((h, r), sl) in enumerate(zip(units, cols)):
        den_floor = jnp.broadcast_to(e_neg_m[r][:, h:h + 1], (chunk, hd))
        hh = tot[u][:, 0:hd] / jnp.maximum(jnp.abs(tot[u][:, hd:2 * hd]), den_floor)
        mu = jnp.mean(hh, axis=-1, keepdims=True)
        dev = hh - mu
        var = jnp.mean(dev * dev, axis=-1, keepdims=True)
        hn = dev * lax.rsqrt(var + EPS) * gain_ref[:, sl]
        o_ref[r, :, sl] = ((hn + skip_ref[:, sl] * xc[r][:, sl]) * _silu(zm[r][:, sl])).astype(o_ref.dtype)


def _mlstm(x, norm_gain, w_in, conv_w, conv_b, wq, wk, wv, w_if, b_if, gain, skip, heads):
    b, l, d = x.shape
    width = conv_w.shape[1]
    hd = width // heads
    chunk = MLSTM_CHUNK
    def block_diag(w):
        nb, d, e = w.shape
        tiled = jnp.tile(w.reshape(nb * d, e), (1, nb))
        same_block = (jnp.arange(nb * d) // d)[:, None] == (jnp.arange(nb * e) // e)[None, :]
        return jnp.where(same_block, tiled, 0.0).astype(BF16)

    wif_pad = (jnp.zeros((w_if.shape[0], 2 * LANES), F32).at[:, :heads].set(w_if[:, :heads])
               .at[:, LANES:LANES + heads].set(w_if[:, heads:])).astype(BF16)
    bif_pad = (jnp.zeros((1, 2 * LANES), F32).at[0, :heads].set(b_if[:heads])
               .at[0, LANES:LANES + heads].set(b_if[heads:]))
    kern = functools.partial(_mlstm_kernel, heads=heads)
    return pl.pallas_call(
        kern,
        grid=(l // chunk,),
        in_specs=[pl.BlockSpec((b, chunk, d), lambda i: (0, i, 0)), _const_spec((1, d)), _const_spec(w_in.shape),
                  _const_spec(conv_w.shape), _const_spec((1, width)),
                  _const_spec((width, width)), _const_spec((width, width)), _const_spec((width, width)),
                  _const_spec((3 * width, 2 * LANES)), _const_spec((1, 2 * LANES)),
                  _const_spec((1, width)), _const_spec((1, width))],
        out_specs=pl.BlockSpec((b, chunk, width), lambda i: (0, i, 0)),
        out_shape=jax.ShapeDtypeStruct((b, l, width), BF16),
        scratch_shapes=[pltpu.VMEM((b, chunk + SUBLANES, width), F32),
                        pltpu.VMEM((b, heads, hd, 2 * hd), F32),
                        pltpu.VMEM((b, SUBLANES, LANES), F32)],
        compiler_params=_params(1),
        name="mlstm",
    )(x, norm_gain.reshape(1, d), w_in.astype(BF16), conv_w, conv_b.reshape(1, width),
      block_diag(wq), block_diag(wk), block_diag(wv),
      wif_pad, bif_pad, gain.reshape(1, width), skip.reshape(1, width))


def _s5_kernel(x_ref, ng_ref, win_ref, pm_ref, pmt_ref, wb_ref, cst_ref, wc_ref, d_ref, wglu_ref, bglu_ref,
               o_ref, xs, carry):
    n_rows, tm, width = o_ref.shape
    seg = tm // SUBLANES
    n_tiles = wb_ref.shape[0]
    kt = wb_ref.shape[1]
    tiles_per_k = n_tiles // (width // kt)
    xw = 2 * LANES
    rows = range(n_rows)

    @pl.when(pl.program_id(0) == 0)
    def _init():
        carry[...] = jnp.zeros_like(carry)

    u = []
    for r in rows:
        h_b = _rms(x_ref[r], ng_ref[...]).astype(BF16)
        u.append(_dot(_dot(pm_ref[...], h_b).astype(BF16), win_ref[...]))
    for r in rows:
        u_b = u[r].astype(BF16)
        for j in range(n_tiles):
            kk = j // tiles_per_k
            xs[r, :, j * xw:(j + 1) * xw] = _dot(u_b[:, kk * kt:(kk + 1) * kt], wb_ref[j])

    row_id = lax.broadcasted_iota(jnp.int32, (SUBLANES, LANES), 0)

    def recur(r, t, state, tiles, store):
        r8 = t * SUBLANES
        new = []
        for idx, j in enumerate(tiles):
            ar, ai = cst_ref[j, 0], cst_ref[j, 1]
            xr, xi = state[2 * idx], state[2 * idx + 1]
            br = xs[r, pl.ds(r8, SUBLANES), j * xw:j * xw + LANES]
            bi = xs[r, pl.ds(r8, SUBLANES), j * xw + LANES:(j + 1) * xw]
            xr, xi = (ar * xr - ai * xi) + br, (ar * xi + ai * xr) + bi
            if store:
                xs[r, pl.ds(r8, SUBLANES), j * xw:j * xw + LANES] = xr
                xs[r, pl.ds(r8, SUBLANES), j * xw + LANES:(j + 1) * xw] = xi
            new += [xr, xi]
        return new

    n_out = wc_ref.shape[0]
    kc = wc_ref.shape[1]
    for r in rows:
        for j0 in range(0, n_tiles, S5_TILES_PER_PASS):
            tiles = tuple(range(j0, min(j0 + S5_TILES_PER_PASS, n_tiles)))
            state = [jnp.zeros((SUBLANES, LANES), F32) for _ in range(2 * len(tiles))]
            for t in range(seg):
                state = recur(r, t, state, tiles, False)

            starts = []
            for idx, j in enumerate(tiles):
                er, ei = state[2 * idx], state[2 * idx + 1]
                for lvl, k in enumerate((1, 2, 4)):
                    mr, mi = cst_ref[j, 2 + 2 * lvl], cst_ref[j, 3 + 2 * lvl]
                    sr, si = pltpu.roll(er, k, axis=0), pltpu.roll(ei, k, axis=0)
                    er, ei = er + (mr * sr - mi * si), ei + (mr * si + mi * sr)
                pr, pi = cst_ref[j, 8], cst_ref[j, 9]
                cr, ci = carry[r, j, 0], carry[r, j, 1]
                er, ei = er + (pr * cr - pi * ci), ei + (pr * ci + pi * cr)
                carry[r, j, 0] = jnp.broadcast_to(er[SUBLANES - 1:SUBLANES, :], (SUBLANES, LANES))
                carry[r, j, 1] = jnp.broadcast_to(ei[SUBLANES - 1:SUBLANES, :], (SUBLANES, LANES))
                starts.append(jnp.where(row_id == 0, cr, pltpu.roll(er, 1, axis=0)))
                starts.append(jnp.where(row_id == 0, ci, pltpu.roll(ei, 1, axis=0)))
            state = starts
            for t in range(seg):
                state = recur(r, t, state, tiles, True)

        y = jnp.concatenate([_dot(xs[r, :, h * kc:(h + 1) * kc].astype(BF16), wc_ref[h]) for h in range(n_out)],
                            axis=1)
        y = _gelu_tanh(y + d_ref[...] * u[r])
        gate = _sigmoid(_dot(y.astype(BF16), wglu_ref[...]) + bglu_ref[...])
        o_ref[r] = _dot(pmt_ref[...], (y * gate).astype(BF16)).astype(o_ref.dtype)


def _s5_weights(a_re, a_im, log_dt, b_re, b_im, c_re, c_im, seg):
    groups, n_state, p = b_re.shape
    width = groups * p
    gpt = LANES // n_state
    n_tiles = groups // gpt
    dt = jnp.exp(log_dt)[:, None]
    mag = jnp.exp(a_re * dt)
    ab_re = mag * jnp.cos(a_im * dt)
    ab_im = mag * jnp.sin(a_im * dt)
    inv = 1.0 / (a_re * a_re + a_im * a_im)
    g_re = ((ab_re - 1.0) * a_re + ab_im * a_im) * inv
    g_im = (ab_im * a_re - (ab_re - 1.0) * a_im) * inv
    bb_re = g_re[..., None] * b_re - g_im[..., None] * b_im
    bb_im = g_re[..., None] * b_im + g_im[..., None] * b_re
    kt = min(MXU_TILE, width)
    n_k = width // kt
    tiles_per_k = n_tiles // n_k
    xw = 2 * LANES

    bb = jnp.stack([bb_re, bb_im], axis=0)
    r_in = jnp.transpose(bb, (1, 3, 0, 2))
    r_in = jnp.broadcast_to(r_in[:, :, :, None, :], (groups, p, 2, gpt, n_state)).reshape(width, xw)
    row_g = jnp.arange(width) // p
    col_gg = (jnp.arange(xw) % LANES) // n_state
    dense_in = jnp.where((row_g % gpt)[:, None] == col_gg[None, :], r_in, 0.0).reshape(n_k, kt, xw)
    tile_id = jnp.arange(n_tiles)
    row_tile = (row_g // gpt).reshape(n_k, kt)[tile_id // tiles_per_k]
    wb = jnp.where(row_tile[:, :, None] == tile_id[:, None, None], dense_in[tile_id // tiles_per_k], 0.0)

    cc = jnp.stack([c_re, -c_im], axis=0)
    r_out = jnp.transpose(cc, (1, 0, 3, 2)).reshape(n_tiles, gpt, 2, n_state, p)
    r_out = jnp.transpose(r_out, (0, 2, 1, 3, 4)).reshape(n_tiles * xw, p)
    r_out = jnp.tile(r_out, (1, kt // p))
    out_row_g = (jnp.arange(n_tiles * xw) // xw) * gpt + (jnp.arange(n_tiles * xw) % LANES) // n_state
    out_col_g = jnp.arange(kt) // p
    wc = jnp.where((out_row_g % (kt // p))[:, None] == out_col_g[None, :], r_out, 0.0)
    wc = wc.reshape(n_k, tiles_per_k * xw, kt)

    a1 = (ab_re.reshape(n_tiles, LANES), ab_im.reshape(n_tiles, LANES))

    def cmul(x, y):
        return x[0] * y[0] - x[1] * y[1], x[0] * y[1] + x[1] * y[0]

    a_seg = a1
    assert seg & (seg - 1) == 0, "segment length must be a power of two"
    for _ in range(seg.bit_length() - 1):
        a_seg = cmul(a_seg, a_seg)
    pows = [a_seg]
    for _ in range(SUBLANES - 1):
        pows.append(cmul(pows[-1], a_seg))
    row = jnp.arange(SUBLANES)[None, :, None]
    planes = [jnp.broadcast_to(a1[c][:, None, :], (n_tiles, SUBLANES, LANES)) for c in range(2)]
    for k in (1, 2, 4):
        for c in range(2):
            planes.append(jnp.where(row >= k, pows[k - 1][c][:, None, :], 0.0))
    for c in range(2):
        planes.append(jnp.stack([pw[c] for pw in pows], axis=1))
    cst = jnp.stack(planes, axis=1).astype(F32)
    return wb.astype(BF16), cst, wc.astype(BF16)


def _s5(x, norm_gain, w_in, a_re, a_im, log_dt, b_re, b_im, c_re, c_im, d_skip, w_glu, b_glu):
    b, l, d = x.shape
    width = d_skip.shape[0]
    tm = S5_TILE
    seg = tm // SUBLANES
    wb, cst, wc = _s5_weights(a_re, a_im, log_dt, b_re, b_im, c_re, c_im, seg)
    n_tiles = wb.shape[0]
    rp = jnp.arange(tm)
    pm = ((rp % SUBLANES) * seg + rp // SUBLANES)[:, None] == rp[None, :]
    return pl.pallas_call(
        _s5_kernel,
        grid=(l // tm,),
        in_specs=[pl.BlockSpec((b, tm, d), lambda i: (0, i, 0)), _const_spec((1, d)), _const_spec(w_in.shape),
                  _const_spec((tm, tm)), _const_spec((tm, tm)),
                  _const_spec(wb.shape), _const_spec(cst.shape), _const_spec(wc.shape),
                  _const_spec((1, width)), _const_spec((width, width)), _const_spec((1, width))],
        out_specs=pl.BlockSpec((b, tm, width), lambda i: (0, i, 0)),
        out_shape=jax.ShapeDtypeStruct((b, l, width), BF16),
        scratch_shapes=[pltpu.VMEM((b, tm, n_tiles * 2 * LANES), F32),
                        pltpu.VMEM((b, n_tiles, 2, SUBLANES, LANES), F32)],
        compiler_params=_params(1),
        name="s5",
    )(x, norm_gain.reshape(1, d), w_in.astype(BF16), pm.astype(BF16), pm.T.astype(BF16), wb, cst, wc,
      d_skip.reshape(1, width), w_glu.astype(BF16),
      b_glu.reshape(1, width))


def _hgrn2_chunk(p_ref, r0, lb, st, causal, o_ref, *, heads, chunk):
    n_rows, _, width = o_ref.shape
    e = width // heads
    rows = pl.ds(r0, chunk)
    units = [(h, r) for h in range(heads) for r in range(n_rows)]

    q_inter, k_state, q_intra, v_b, gate, decay = [], [], [], [], [], []
    for h, r in units:
        sl = slice(h * e, (h + 1) * e)
        q = p_ref[r, rows, h * e:(h + 1) * e]
        f = p_ref[r, rows, width + h * e:width + (h + 1) * e]
        lb_h = lb[:, sl]
        fg = lb_h + (1.0 - lb_h) * _sigmoid(f)
        b = _cumsum_rows(jnp.log(fg))
        b_last = b[chunk - 1:chunk, :]
        q_inter.append((q * jnp.exp(b)).astype(BF16))
        k_state.append(((1.0 - fg) * jnp.exp(b_last - b)).astype(BF16))
        q_intra.append((q * jnp.exp(b - b_last)).astype(BF16))
        v_b.append(p_ref[r, rows, 2 * width + h * e:2 * width + (h + 1) * e].astype(BF16))
        gate.append(p_ref[r, rows, 3 * width + h * e:3 * width + (h + 1) * e])
        decay.append(jnp.exp(b_last))

    scores = [_dot_nt(q_intra[u], k_state[u]) for u in range(len(units))]
    scores = [jnp.where(causal, sc, 0.0).astype(BF16) for sc in scores]
    outs = []
    for u, (h, r) in enumerate(units):
        s_t = st[r, h]
        outs.append(_dot(scores[u], v_b[u]) + _dot_nt(q_inter[u], s_t.astype(BF16)))
        st[r, h] = s_t * decay[u] + _dot_tn(v_b[u], k_state[u])
    for u, (h, r) in enumerate(units):
        o = outs[u]
        o = o * lax.rsqrt(jnp.mean(o * o, axis=-1, keepdims=True) + EPS)
        o_ref[r, rows, h * e:(h + 1) * e] = (o * gate[u]).astype(o_ref.dtype)


def _causal_mask(chunk):
    ti = lax.broadcasted_iota(jnp.int32, (chunk, chunk), 0)
    si = lax.broadcasted_iota(jnp.int32, (chunk, chunk), 1)
    return si <= ti


def _tail_math(xs, mix_groups, wo_ref, g1_ref, g2_ref, g3_ref, w1_ref, w3_ref, w2_ref):
    mps = []
    for mix_refs in mix_groups:
        mp = None
        row0 = 0
        for m_ref in mix_refs:
            part = _dot(m_ref[...], wo_ref[row0:row0 + m_ref.shape[1], :])
            mp = part if mp is None else mp + part
            row0 += m_ref.shape[1]
        mps.append(mp)
    x1s = [x + _rms(mp, g1_ref[...]) for x, mp in zip(xs, mps)]
    hs = [_rms(x1, g2_ref[...]).astype(BF16) for x1 in x1s]
    ys = []
    for h in hs:
        z = (_silu(_dot(h, w1_ref[...])) * _dot(h, w3_ref[...])).astype(BF16)
        ys.append(_dot(z, w2_ref[...]))
    return [x1 + _rms(y, g3_ref[...]) for x1, y in zip(x1s, ys)]


def _hgrn2_kernel(p_ref, lb_ref, o_ref, st, *, heads, chunk):
    tm = o_ref.shape[1]

    @pl.when(pl.program_id(0) == 0)
    def _init():
        st[...] = jnp.zeros_like(st)

    lb = lb_ref[...]
    causal = _causal_mask(chunk)

    def body(c, _):
        _hgrn2_chunk(p_ref, pl.multiple_of(c * chunk, chunk), lb, st, causal, o_ref, heads=heads, chunk=chunk)
        return 0

    lax.fori_loop(0, tm // chunk, body, 0, unroll=HGRN_CHUNK_UNROLL)


def _hgrn2(proj, lb):
    b, l, w4 = proj.shape
    width = w4 // 4
    heads = width // H_EXPAND
    tm = HGRN_TILE
    kern = functools.partial(_hgrn2_kernel, heads=heads, chunk=HGRN_CHUNK)
    return pl.pallas_call(
        kern,
        grid=(l // tm,),
        in_specs=[pl.BlockSpec((b, tm, w4), lambda i: (0, i, 0)), _const_spec((1, width))],
        out_specs=pl.BlockSpec((b, tm, width), lambda i: (0, i, 0)),
        out_shape=jax.ShapeDtypeStruct((b, l, width), BF16),
        scratch_shapes=[pltpu.VMEM((b, heads, H_EXPAND, H_EXPAND), F32)],
        compiler_params=_params(1),
        name="hgrn2",
    )(proj, lb.reshape(1, width))


def _tail_kernel(*refs, n_mix):
    x_ref = refs[0]
    mix_refs = refs[1:1 + n_mix]
    wo_ref, g1_ref, g2_ref, g3_ref, w1_ref, w3_ref, w2_ref, o_ref = refs[1 + n_mix:]
    rows = o_ref.shape[0] // TAIL_ROW_GROUPS
    groups = [slice(r * rows, (r + 1) * rows) for r in range(TAIL_ROW_GROUPS)]
    outs = _tail_math([x_ref[sl, :] for sl in groups], [[m.at[sl, :] for m in mix_refs] for sl in groups],
                      wo_ref, g1_ref, g2_ref, g3_ref, w1_ref, w3_ref, w2_ref)
    for sl, out in zip(groups, outs):
        o_ref[sl, :] = out


def _block_tail(x2d, mixes, w_out, g1, g2, g3, w1, w3, w2):
    t, d = x2d.shape
    f = w1.shape[1]
    tm = ROW_TILE_TAIL
    n_mix = len(mixes)
    row = lambda i: (i, 0)
    in_specs = ([pl.BlockSpec((tm, d), row)]
                + [pl.BlockSpec((tm, m.shape[1]), row) for m in mixes]
                + [_const_spec(w_out.shape)]
                + [_const_spec((1, d))] * 3
                + [_const_spec((d, f)), _const_spec((d, f)), _const_spec((f, d))])
    return pl.pallas_call(
        functools.partial(_tail_kernel, n_mix=n_mix),
        grid=(t // tm,),
        in_specs=in_specs,
        out_specs=pl.BlockSpec((tm, d), row),
        out_shape=jax.ShapeDtypeStruct((t, d), F32),
        compiler_params=_params(1),
        name="outproj_swiglu",
    )(x2d, *mixes, w_out.astype(BF16), g1.reshape(1, d), g2.reshape(1, d), g3.reshape(1, d),
      w1.astype(BF16), w3.astype(BF16), w2.astype(BF16))


def kernel(x, norm_g, ab_w_in, ab_conv_w, ab_conv_b, ab_wq, ab_wk, ab_wv, ab_w_if, ab_b_if, ab_mh_gain, ab_skip,
           ab_a_re, ab_a_im, ab_log_dt, ab_b_re, ab_b_im, ab_c_re, ab_c_im, ab_d, ab_w_glu, ab_b_glu, ab_w_out,
           c_w_in, c_lb_raw, c_g_gain, c_w_out, ffn_w1, ffn_w3, ffn_w2):
    bsz, seq, d = x.shape
    t = bsz * seq
    depth = norm_g.shape[0]
    x2d = x.reshape(t, d)
    lbs = jnp.cumsum(jax.nn.softmax(c_lb_raw.astype(F32), axis=0), axis=0)
    for layer in range(depth):
        j = layer // 2
        if layer % 2 == 0:
            m_width = ab_conv_w.shape[2]
            s_width = ab_d.shape[1]
            x3d = x2d.reshape(bsz, seq, d)
            out_m = _mlstm(x3d, norm_g[layer, 0], ab_w_in[j][:, :2 * m_width], ab_conv_w[j], ab_conv_b[j], ab_wq[j],
                           ab_wk[j], ab_wv[j], ab_w_if[j], ab_b_if[j], ab_mh_gain[j], ab_skip[j], M_HEADS)
            out_s = _s5(x3d, norm_g[layer, 0], ab_w_in[j][:, 2 * m_width:], ab_a_re[j], ab_a_im[j], ab_log_dt[j],
                        ab_b_re[j], ab_b_im[j], ab_c_re[j], ab_c_im[j], ab_d[j], ab_w_glu[j], ab_b_glu[j])
            mixes = [out_m.reshape(t, m_width), out_s.reshape(t, s_width)]
            w_out = ab_w_out[j]
        else:
            proj = _hgrn2_proj(x2d, norm_g[layer, 0], c_w_in[j].astype(BF16), c_g_gain[j], ROW_TILE_PROJ)
            lb = lbs[layer] - lbs[0]
            mixes = [_hgrn2(proj.reshape(bsz, seq, -1), lb).reshape(t, -1)]
            w_out = c_w_out[j]
        x2d = _block_tail(x2d, mixes, w_out, norm_g[layer, 1], norm_g[layer, 2], norm_g[layer, 3],
                          ffn_w1[layer], ffn_w3[layer], ffn_w2[layer])
    return x2d.reshape(bsz, seq, d)
```

```python
import functools
import math

import jax
import jax.numpy as jnp
from jax import lax
from jax.experimental import pallas as pl
from jax.experimental.pallas import tpu as pltpu

F32 = jnp.float32
BF16 = jnp.bfloat16
EPS = 1e-6

V7X_VMEM_LIMIT_BYTES = 56 * 1024 * 1024
SUBLANES = 8
LANES = 128
MXU_TILE = 256

M_HEADS = 4
M_CONV = 4
QKV_BLOCK = 4
S_GROUP = 16
S_STATE = 64
H_EXPAND = 128

ROW_TILE_PROJ = 1024
ROW_TILE_TAIL = 1024
MLSTM_CHUNK = 256
S5_TILE = 256
S5_TILES_PER_PASS = 8
HGRN_TILE = 512
HGRN_CHUNK_UNROLL = 2
TAIL_ROW_GROUPS = 4
HGRN_CHUNK = 64


def _dot(a, b):
    return jnp.dot(a, b, preferred_element_type=F32)


def _dot_nt(a, b):
    return lax.dot_general(a, b, (((1,), (1,)), ((), ())), preferred_element_type=F32)


def _dot_tn(a, b):
    return lax.dot_general(a, b, (((0,), (0,)), ((), ())), preferred_element_type=F32)


def _sigmoid(x):
    return 1.0 / (1.0 + jnp.exp(-x))


def _silu(x):
    return x * _sigmoid(x)


def _log_sigmoid(x):
    return jnp.minimum(x, 0.0) - jnp.log(1.0 + jnp.exp(-jnp.abs(x)))


def _gelu_tanh(x):
    return 0.5 * x * (1.0 + jnp.tanh(math.sqrt(2.0 / math.pi) * (x + 0.044715 * (x * x * x))))


def _rms(x, g):
    return x * lax.rsqrt(jnp.mean(x * x, axis=-1, keepdims=True) + EPS) * g


def _prefix_rows(x, op, identity):
    rows, n = x.shape
    row = lax.broadcasted_iota(jnp.int32, (SUBLANES, n), 0)
    outs = []
    carry = None
    for r in range(rows // SUBLANES):
        t = x[r * SUBLANES:(r + 1) * SUBLANES, :]
        for k in (1, 2, 4):
            t = op(t, jnp.where(row >= k, pltpu.roll(t, k, axis=0), identity))
        if carry is not None:
            t = op(t, carry)
        carry = t[SUBLANES - 1:SUBLANES, :]
        outs.append(t)
    return jnp.concatenate(outs, axis=0)


def _cumsum_rows(x):
    return _prefix_rows(x, jnp.add, 0.0)


def _cummax_rows(x):
    return _prefix_rows(x, jnp.maximum, -jnp.inf)


def _params(n_axes):
    return pltpu.CompilerParams(dimension_semantics=("arbitrary",) * n_axes,
                                vmem_limit_bytes=V7X_VMEM_LIMIT_BYTES)


def _const_spec(shape):
    nd = len(shape)
    return pl.BlockSpec(shape, lambda *_: (0,) * nd, pipeline_mode=pl.Buffered(1))


def _hgrn2_proj_kernel(x_ref, g_ref, w_ref, gain_ref, o_ref):
    width = gain_ref.shape[1]
    h = _rms(x_ref[...], g_ref[...]).astype(BF16)
    o_ref[:, 0:width] = _silu(_dot(h, w_ref[:, 0:width]))
    o_ref[:, width:3 * width] = _dot(h, w_ref[:, width:3 * width])
    o_ref[:, 3 * width:4 * width] = gain_ref[...] * _silu(_dot(h, w_ref[:, 3 * width:4 * width]))


def _hgrn2_proj(x2d, g, w, gain, tm):
    t, d = x2d.shape
    n = w.shape[1]
    width = n // 4
    return pl.pallas_call(
        _hgrn2_proj_kernel,
        grid=(t // tm,),
        in_specs=[pl.BlockSpec((tm, d), lambda i: (i, 0)), _const_spec((1, d)), _const_spec((d, n)),
                  _const_spec((1, width))],
        out_specs=pl.BlockSpec((tm, n), lambda i: (i, 0)),
        out_shape=jax.ShapeDtypeStruct((t, n), F32),
        compiler_params=_params(1),
        name="hgrn2_proj",
    )(x2d, g.reshape(1, d), w, gain.reshape(1, width))


def _mlstm_kernel(x_ref, ng_ref, win_ref, convw_ref, convb_ref, wq_ref, wk_ref, wv_ref, wif_ref, bif_ref, gain_ref,
                  skip_ref, o_ref, xbuf, c_s, m_s, *, heads):
    n_rows, chunk, width = o_ref.shape
    hd = width // heads
    taps = convw_ref.shape[0]
    rows = range(n_rows)

    @pl.when(pl.program_id(0) == 0)
    def _init():
        xbuf[:, 0:SUBLANES, :] = jnp.zeros((n_rows, SUBLANES, width), F32)
        c_s[...] = jnp.zeros_like(c_s)
        m_s[...] = jnp.zeros_like(m_s)

    proj = [_dot(_rms(x_ref[r], ng_ref[...]).astype(BF16), win_ref[...]) for r in rows]
    xm = [p[:, 0:width] for p in proj]
    zm = [p[:, width:2 * width] for p in proj]

    xc = []
    for r in rows:
        xbuf[r, SUBLANES:SUBLANES + chunk, :] = xm[r]
        conv = convb_ref[...] + convw_ref[taps - 1:taps, :] * xm[r]
        for k in range(taps - 1):
            conv = conv + convw_ref[k:k + 1, :] * xbuf[r, pl.ds(SUBLANES - (taps - 1) + k, chunk), :]
        xbuf[r, 0:SUBLANES, :] = xm[r][chunk - SUBLANES:chunk, :]
        xc.append(_silu(conv))

    q_b, k, k_b, v_b, gates = [], [], [], [], []
    for r in rows:
        xc_b = xc[r].astype(BF16)
        q_b.append(_dot(xc_b, wq_ref[...]).astype(BF16))
        k.append(_dot(xc_b, wk_ref[...]) * (hd ** -0.5))
        k_b.append(k[r].astype(BF16))
        v_b.append(_dot(xm[r].astype(BF16), wv_ref[...]).astype(BF16))
        gates.append(_dot(q_b[r], wif_ref[0:width, :]) + _dot(k_b[r], wif_ref[width:2 * width, :])
                     + _dot(v_b[r], wif_ref[2 * width:3 * width, :]) + bif_ref[...])

    m_prev, row_term, e_neg_m, k_scale, s_old, s_new, g_rows = [], [], [], [], [], [], []
    for r in rows:
        log_i = gates[r][:, 0:LANES]
        b = _cumsum_rows(_log_sigmoid(gates[r][:, LANES:2 * LANES]))
        g = log_i - b
        g_max = _cummax_rows(g)
        m_prev.append(m_s[r, 0:1, :])
        m_t = jnp.maximum(b + m_prev[r], b + g_max)
        row_term.append(b - m_t)
        e_neg_m.append(jnp.exp(-m_t))
        b_last = b[chunk - 1:chunk, :]
        a_max = b_last + g_max[chunk - 1:chunk, :]
        k_scale.append(jnp.exp(g - g_max[chunk - 1:chunk, :]))
        m_new = jnp.maximum(b_last + m_prev[r], a_max)
        s_old.append(jnp.exp(b_last + m_prev[r] - m_new))
        s_new.append(jnp.exp(a_max - m_new))
        m_s[r] = jnp.broadcast_to(m_new, m_s.shape[1:])
        g_rows.append(g.T)

    causal = _causal_mask(chunk)
    ones_b = jnp.ones((chunk, hd), BF16)

    units = [(h, r) for h in range(heads) for r in rows]
    cols = [slice(h * hd, (h + 1) * hd) for h, _ in units]

    v_ones = [jnp.concatenate([v_b[r][:, sl], ones_b], axis=1) for (h, r), sl in zip(units, cols)]
    qk = [_dot_nt(q_b[r][:, sl], k_b[r][:, sl]) for (h, r), sl in zip(units, cols)]
    inter = [_dot(q_b[r][:, sl], c_s[r, h].astype(BF16)) for (h, r), sl in zip(units, cols)]

    scores, s_inter = [], []
    for u, (h, r) in enumerate(units):
        row_b = jnp.broadcast_to(row_term[r][:, h:h + 1], (chunk, hd))
        row_w = jnp.concatenate([row_b] * (chunk // hd), axis=1)
        w_intra = jnp.exp(jnp.where(causal, row_w + g_rows[r][h:h + 1, :], -jnp.inf))
        s_inter.append(jnp.exp(row_b + m_prev[r][:, h:h + 1]))
        scores.append((qk[u] * w_intra).astype(BF16))

    tot = [_dot(scores[u], v_ones[u]) + jnp.concatenate([s_inter[u], s_inter[u]], axis=1) * inter[u]
           for u in range(len(units))]

    for u, ((h, r), sl) in enumerate(zip(units, cols)):
        kw = (k[r][:, sl] * jnp.broadcast_to(k_scale[r][:, h:h + 1], (chunk, hd))).astype(BF16)
        c_s[r, h] = s_old[r][:, h:h + 1] * c_s[r, h] + s_new[r][:, h:h + 1] * _dot_tn(kw, v_ones[u])

    for u, ((h, r), sl) in enumerate(zip(units, cols)):
        den_floor = jnp.broadcast_to(e_neg_m[r][:, h:h + 1], (chunk, hd))
        hh = tot[u][:, 0:hd] / jnp.maximum(jnp.abs(tot[u][:, hd:2 * hd]), den_floor)
        mu = jnp.mean(hh, axis=-1, keepdims=True)
        dev = hh - mu
        var = jnp.mean(dev * dev, axis=-1, keepdims=True)
        hn = dev * lax.rsqrt(var + EPS) * gain_ref[:, sl]
        o_ref[r, :, sl] = ((hn + skip_ref[:, sl] * xc[r][:, sl]) * _silu(zm[r][:, sl])).astype(o_ref.dtype)


def _mlstm(x, norm_gain, w_in, conv_w, conv_b, wq, wk, wv, w_if, b_if, gain, skip, heads):
    b, l, d = x.shape
    width = conv_w.shape[1]
    hd = width // heads
    chunk = MLSTM_CHUNK
    def block_diag(w):
        nb, d, e = w.shape
        tiled = jnp.tile(w.reshape(nb * d, e), (1, nb))
        same_block = (jnp.arange(nb * d) // d)[:, None] == (jnp.arange(nb * e) // e)[None, :]
        return jnp.where(same_block, tiled, 0.0).astype(BF16)

    wif_pad = (jnp.zeros((w_if.shape[0], 2 * LANES), F32).at[:, :heads].set(w_if[:, :heads])
               .at[:, LANES:LANES + heads].set(w_if[:, heads:])).astype(BF16)
    bif_pad = (jnp.zeros((1, 2 * LANES), F32).at[0, :heads].set(b_if[:heads])
               .at[0, LANES:LANES + heads].set(b_if[heads:]))
    kern = functools.partial(_mlstm_kernel, heads=heads)
    return pl.pallas_call(
        kern,
        grid=(l // chunk,),
        in_specs=[pl.BlockSpec((b, chunk, d), lambda i: (0, i, 0)), _const_spec((1, d)), _const_spec(w_in.shape),
                  _const_spec(conv_w.shape), _const_spec((1, width)),
                  _const_spec((width, width)), _const_spec((width, width)), _const_spec((width, width)),
                  _const_spec((3 * width, 2 * LANES)), _const_spec((1, 2 * LANES)),
                  _const_spec((1, width)), _const_spec((1, width))],
        out_specs=pl.BlockSpec((b, chunk, width), lambda i: (0, i, 0)),
        out_shape=jax.ShapeDtypeStruct((b, l, width), BF16),
        scratch_shapes=[pltpu.VMEM((b, chunk + SUBLANES, width), F32),
                        pltpu.VMEM((b, heads, hd, 2 * hd), F32),
                        pltpu.VMEM((b, SUBLANES, LANES), F32)],
        compiler_params=_params(1),
        name="mlstm",
    )(x, norm_gain.reshape(1, d), w_in.astype(BF16), conv_w, conv_b.reshape(1, width),
      block_diag(wq), block_diag(wk), block_diag(wv),
      wif_pad, bif_pad, gain.reshape(1, width), skip.reshape(1, width))


def _s5_kernel(x_ref, ng_ref, win_ref, pm_ref, pmt_ref, wb_ref, cst_ref, wc_ref, d_ref, wglu_ref, bglu_ref,
               o_ref, xs, carry):
    n_rows, tm, width = o_ref.shape
    seg = tm // SUBLANES
    n_tiles = wb_ref.shape[0]
    kt = wb_ref.shape[1]
    tiles_per_k = n_tiles // (width // kt)
    xw = 2 * LANES
    rows = range(n_rows)

    @pl.when(pl.program_id(0) == 0)
    def _init():
        carry[...] = jnp.zeros_like(carry)

    u = []
    for r in rows:
        h_b = _rms(x_ref[r], ng_ref[...]).astype(BF16)
        u.append(_dot(_dot(pm_ref[...], h_b).astype(BF16), win_ref[...]))
    for r in rows:
        u_b = u[r].astype(BF16)
        for j in range(n_tiles):
            kk = j // tiles_per_k
            xs[r, :, j * xw:(j + 1) * xw] = _dot(u_b[:, kk * kt:(kk + 1) * kt], wb_ref[j])

    row_id = lax.broadcasted_iota(jnp.int32, (SUBLANES, LANES), 0)

    def recur(r, t, state, tiles, store):
        r8 = t * SUBLANES
        new = []
        for idx, j in enumerate(tiles):
            ar, ai = cst_ref[j, 0], cst_ref[j, 1]
            xr, xi = state[2 * idx], state[2 * idx + 1]
            br = xs[r, pl.ds(r8, SUBLANES), j * xw:j * xw + LANES]
            bi = xs[r, pl.ds(r8, SUBLANES), j * xw + LANES:(j + 1) * xw]
            xr, xi = (ar * xr - ai * xi) + br, (ar * xi + ai * xr) + bi
            if store:
                xs[r, pl.ds(r8, SUBLANES), j * xw:j * xw + LANES] = xr
                xs[r, pl.ds(r8, SUBLANES), j * xw + LANES:(j + 1) * xw] = xi
            new += [xr, xi]
        return new

    n_out = wc_ref.shape[0]
    kc = wc_ref.shape[1]
    for r in rows:
        for j0 in range(0, n_tiles, S5_TILES_PER_PASS):
            tiles = tuple(range(j0, min(j0 + S5_TILES_PER_PASS, n_tiles)))
            state = [jnp.zeros((SUBLANES, LANES), F32) for _ in range(2 * len(tiles))]
            for t in range(seg):
                state = recur(r, t, state, tiles, False)

            starts = []
            for idx, j in enumerate(tiles):
                er, ei = state[2 * idx], state[2 * idx + 1]
                for lvl, k in enumerate((1, 2, 4)):
                    mr, mi = cst_ref[j, 2 + 2 * lvl], cst_ref[j, 3 + 2 * lvl]
                    sr, si = pltpu.roll(er, k, axis=0), pltpu.roll(ei, k, axis=0)
                    er, ei = er + (mr * sr - mi * si), ei + (mr * si + mi * sr)
                pr, pi = cst_ref[j, 8], cst_ref[j, 9]
                cr, ci = carry[r, j, 0], carry[r, j, 1]
                er, ei = er + (pr * cr - pi * ci), ei + (pr * ci + pi * cr)
                carry[r, j, 0] = jnp.broadcast_to(er[SUBLANES - 1:SUBLANES, :], (SUBLANES, LANES))
                carry[r, j, 1] = jnp.broadcast_to(ei[SUBLANES - 1:SUBLANES, :], (SUBLANES, LANES))
                starts.append(jnp.where(row_id == 0, cr, pltpu.roll(er, 1, axis=0)))
                starts.append(jnp.where(row_id == 0, ci, pltpu.roll(ei, 1, axis=0)))
            state = starts
            for t in range(seg):
                state = recur(r, t, state, tiles, True)

        y = jnp.concatenate([_dot(xs[r, :, h * kc:(h + 1) * kc].astype(BF16), wc_ref[h]) for h in range(n_out)],
                            axis=1)
        y = _gelu_tanh(y + d_ref[...] * u[r])
        gate = _sigmoid(_dot(y.astype(BF16), wglu_ref[...]) + bglu_ref[...])
        o_ref[r] = _dot(pmt_ref[...], (y * gate).astype(BF16)).astype(o_ref.dtype)


def _s5_weights(a_re, a_im, log_dt, b_re, b_im, c_re, c_im, seg):
    groups, n_state, p = b_re.shape
    width = groups * p
    gpt = LANES // n_state
    n_tiles = groups // gpt
    dt = jnp.exp(log_dt)[:, None]
    mag = jnp.exp(a_re * dt)
    ab_re = mag * jnp.cos(a_im * dt)
    ab_im = mag * jnp.sin(a_im * dt)
    inv = 1.0 / (a_re * a_re + a_im * a_im)
    g_re = ((ab_re - 1.0) * a_re + ab_im * a_im) * inv
    g_im = (ab_im * a_re - (ab_re - 1.0) * a_im) * inv
    bb_re = g_re[..., None] * b_re - g_im[..., None] * b_im
    bb_im = g_re[..., None] * b_im + g_im[..., None] * b_re
    kt = min(MXU_TILE, width)
    n_k = width // kt
    tiles_per_k = n_tiles // n_k
    xw = 2 * LANES

    bb = jnp.stack([bb_re, bb_im], axis=0)
    r_in = jnp.transpose(bb, (1, 3, 0, 2))
    r_in = jnp.broadcast_to(r_in[:, :, :, None, :], (groups, p, 2, gpt, n_state)).reshape(width, xw)
    row_g = jnp.arange(width) // p
    col_gg = (jnp.arange(xw) % LANES) // n_state
    dense_in = jnp.where((row_g % gpt)[:, None] == col_gg[None, :], r_in, 0.0).reshape(n_k, kt, xw)
    tile_id = jnp.arange(n_tiles)
    row_tile = (row_g // gpt).reshape(n_k, kt)[tile_id // tiles_per_k]
    wb = jnp.where(row_tile[:, :, None] == tile_id[:, None, None], dense_in[tile_id // tiles_per_k], 0.0)

    cc = jnp.stack([c_re, -c_im], axis=0)
    r_out = jnp.transpose(cc, (1, 0, 3, 2)).reshape(n_tiles, gpt, 2, n_state, p)
    r_out = jnp.transpose(r_out, (0, 2, 1, 3, 4)).reshape(n_tiles * xw, p)
    r_out = jnp.tile(r_out, (1, kt // p))
    out_row_g = (jnp.arange(n_tiles * xw) // xw) * gpt + (jnp.arange(n_tiles * xw) % LANES) // n_state
    out_col_g = jnp.arange(kt) // p
    wc = jnp.where((out_row_g % (kt // p))[:, None] == out_col_g[None, :], r_out, 0.0)
    wc = wc.reshape(n_k, tiles_per_k * xw, kt)

    a1 = (ab_re.reshape(n_tiles, LANES), ab_im.reshape(n_tiles, LANES))

    def cmul(x, y):
        return x[0] * y[0] - x[1] * y[1], x[0] * y[1] + x[1] * y[0]

    a_seg = a1
    assert seg & (seg - 1) == 0, "segment length must be a power of two"
    for _ in range(seg.bit_length() - 1):
        a_seg = cmul(a_seg, a_seg)
    pows = [a_seg]
    for _ in range(SUBLANES - 1):
        pows.append(cmul(pows[-1], a_seg))
    row = jnp.arange(SUBLANES)[None, :, None]
    planes = [jnp.broadcast_to(a1[c][:, None, :], (n_tiles, SUBLANES, LANES)) for c in range(2)]
    for k in (1, 2, 4):
        for c in range(2):
            planes.append(jnp.where(row >= k, pows[k - 1][c][:, None, :], 0.0))
    for c in range(2):
        planes.append(jnp.stack([pw[c] for pw in pows], axis=1))
    cst = jnp.stack(planes, axis=1).astype(F32)
    return wb.astype(BF16), cst, wc.astype(BF16)


def _s5(x, norm_gain, w_in, a_re, a_im, log_dt, b_re, b_im, c_re, c_im, d_skip, w_glu, b_glu):
    b, l, d = x.shape
    width = d_skip.shape[0]
    tm = S5_TILE
    seg = tm // SUBLANES
    wb, cst, wc = _s5_weights(a_re, a_im, log_dt, b_re, b_im, c_re, c_im, seg)
    n_tiles = wb.shape[0]
    rp = jnp.arange(tm)
    pm = ((rp % SUBLANES) * seg + rp // SUBLANES)[:, None] == rp[None, :]
    return pl.pallas_call(
        _s5_kernel,
        grid=(l // tm,),
        in_specs=[pl.BlockSpec((b, tm, d), lambda i: (0, i, 0)), _const_spec((1, d)), _const_spec(w_in.shape),
                  _const_spec((tm, tm)), _const_spec((tm, tm)),
                  _const_spec(wb.shape), _const_spec(cst.shape), _const_spec(wc.shape),
                  _const_spec((1, width)), _const_spec((width, width)), _const_spec((1, width))],
        out_specs=pl.BlockSpec((b, tm, width), lambda i: (0, i, 0)),
        out_shape=jax.ShapeDtypeStruct((b, l, width), BF16),
        scratch_shapes=[pltpu.VMEM((b, tm, n_tiles * 2 * LANES), F32),
                        pltpu.VMEM((b, n_tiles, 2, SUBLANES, LANES), F32)],
        compiler_params=_params(1),
        name="s5",
    )(x, norm_gain.reshape(1, d), w_in.astype(BF16), pm.astype(BF16), pm.T.astype(BF16), wb, cst, wc,
      d_skip.reshape(1, width), w_glu.astype(BF16),
      b_glu.reshape(1, width))


def _hgrn2_chunk(p_ref, r0, lb, st, causal, o_ref, *, heads, chunk):
    n_rows, _, width = o_ref.shape
    e = width // heads
    rows = pl.ds(r0, chunk)
    units = [(h, r) for h in range(heads) for r in range(n_rows)]

    q_inter, k_state, q_intra, v_b, gate, decay = [], [], [], [], [], []
    for h, r in units:
        sl = slice(h * e, (h + 1) * e)
        q = p_ref[r, rows, h * e:(h + 1) * e]
        f = p_ref[r, rows, width + h * e:width + (h + 1) * e]
        lb_h = lb[:, sl]
        fg = lb_h + (1.0 - lb_h) * _sigmoid(f)
        b = _cumsum_rows(jnp.log(fg))
        b_last = b[chunk - 1:chunk, :]
        q_inter.append((q * jnp.exp(b)).astype(BF16))
        to_end = jnp.exp(b_last - b)
        k_state.append(((1.0 - fg) * to_end).astype(BF16))
        q_intra.append((q * (1.0 / to_end)).astype(BF16))
        v_b.append(p_ref[r, rows, 2 * width + h * e:2 * width + (h + 1) * e].astype(BF16))
        gate.append(p_ref[r, rows, 3 * width + h * e:3 * width + (h + 1) * e])
        decay.append(jnp.exp(b_last))

    scores = [_dot_nt(q_intra[u], k_state[u]) for u in range(len(units))]
    scores = [jnp.where(causal, sc, 0.0).astype(BF16) for sc in scores]
    outs = []
    for u, (h, r) in enumerate(units):
        s_t = st[r, h]
        outs.append(_dot(scores[u], v_b[u]) + _dot_nt(q_inter[u], s_t.astype(BF16)))
        st[r, h] = s_t * decay[u] + _dot_tn(v_b[u], k_state[u])
    for u, (h, r) in enumerate(units):
        o = outs[u]
        o = o * lax.rsqrt(jnp.mean(o * o, axis=-1, keepdims=True) + EPS)
        o_ref[r, rows, h * e:(h + 1) * e] = (o * gate[u]).astype(o_ref.dtype)


def _causal_mask(chunk):
    ti = lax.broadcasted_iota(jnp.int32, (chunk, chunk), 0)
    si = lax.broadcasted_iota(jnp.int32, (chunk, chunk), 1)
    return si <= ti


def _tail_math(xs, mix_groups, wo_ref, g1_ref, g2_ref, g3_ref, w1_ref, w3_ref, w2_ref):
    mps = []
    for mix_refs in mix_groups:
        mp = None
        row0 = 0
        for m_ref in mix_refs:
            part = _dot(m_ref[...], wo_ref[row0:row0 + m_ref.shape[1], :])
            mp = part if mp is None else mp + part
            row0 += m_ref.shape[1]
        mps.append(mp)
    x1s = [x + _rms(mp, g1_ref[...]) for x, mp in zip(xs, mps)]
    hs = [_rms(x1, g2_ref[...]).astype(BF16) for x1 in x1s]
    ys = []
    for h in hs:
        z = (_silu(_dot(h, w1_ref[...])) * _dot(h, w3_ref[...])).astype(BF16)
        ys.append(_dot(z, w2_ref[...]))
    return [x1 + _rms(y, g3_ref[...]) for x1, y in zip(x1s, ys)]


def _hgrn2_kernel(p_ref, lb_ref, o_ref, st, *, heads, chunk):
    tm = o_ref.shape[1]

    @pl.when(pl.program_id(0) == 0)
    def _init():
        st[...] = jnp.zeros_like(st)

    lb = lb_ref[...]
    causal = _causal_mask(chunk)

    def body(c, _):
        _hgrn2_chunk(p_ref, pl.multiple_of(c * chunk, chunk), lb, st, causal, o_ref, heads=heads, chunk=chunk)
        return 0

    lax.fori_loop(0, tm // chunk, body, 0, unroll=HGRN_CHUNK_UNROLL)


def _hgrn2(proj, lb):
    b, l, w4 = proj.shape
    width = w4 // 4
    heads = width // H_EXPAND
    tm = HGRN_TILE
    kern = functools.partial(_hgrn2_kernel, heads=heads, chunk=HGRN_CHUNK)
    return pl.pallas_call(
        kern,
        grid=(l // tm,),
        in_specs=[pl.BlockSpec((b, tm, w4), lambda i: (0, i, 0)), _const_spec((1, width))],
        out_specs=pl.BlockSpec((b, tm, width), lambda i: (0, i, 0)),
        out_shape=jax.ShapeDtypeStruct((b, l, width), BF16),
        scratch_shapes=[pltpu.VMEM((b, heads, H_EXPAND, H_EXPAND), F32)],
        compiler_params=_params(1),
        name="hgrn2",
    )(proj, lb.reshape(1, width))


def _tail_kernel(*refs, n_mix):
    x_ref = refs[0]
    mix_refs = refs[1:1 + n_mix]
    wo_ref, g1_ref, g2_ref, g3_ref, w1_ref, w3_ref, w2_ref, o_ref = refs[1 + n_mix:]
    rows = o_ref.shape[0] // TAIL_ROW_GROUPS
    groups = [slice(r * rows, (r + 1) * rows) for r in range(TAIL_ROW_GROUPS)]
    outs = _tail_math([x_ref[sl, :] for sl in groups], [[m.at[sl, :] for m in mix_refs] for sl in groups],
                      wo_ref, g1_ref, g2_ref, g3_ref, w1_ref, w3_ref, w2_ref)
    for sl, out in zip(groups, outs):
        o_ref[sl, :] = out


def _block_tail(x2d, mixes, w_out, g1, g2, g3, w1, w3, w2):
    t, d = x2d.shape
    f = w1.shape[1]
    tm = ROW_TILE_TAIL
    n_mix = len(mixes)
    row = lambda i: (i, 0)
    in_specs = ([pl.BlockSpec((tm, d), row)]
                + [pl.BlockSpec((tm, m.shape[1]), row) for m in mixes]
                + [_const_spec(w_out.shape)]
                + [_const_spec((1, d))] * 3
                + [_const_spec((d, f)), _const_spec((d, f)), _const_spec((f, d))])
    return pl.pallas_call(
        functools.partial(_tail_kernel, n_mix=n_mix),
        grid=(t // tm,),
        in_specs=in_specs,
        out_specs=pl.BlockSpec((tm, d), row),
        out_shape=jax.ShapeDtypeStruct((t, d), F32),
        compiler_params=_params(1),
        name="outproj_swiglu",
    )(x2d, *mixes, w_out.astype(BF16), g1.reshape(1, d), g2.reshape(1, d), g3.reshape(1, d),
      w1.astype(BF16), w3.astype(BF16), w2.astype(BF16))


def kernel(x, norm_g, ab_w_in, ab_conv_w, ab_conv_b, ab_wq, ab_wk, ab_wv, ab_w_if, ab_b_if, ab_mh_gain, ab_skip,
           ab_a_re, ab_a_im, ab_log_dt, ab_b_re, ab_b_im, ab_c_re, ab_c_im, ab_d, ab_w_glu, ab_b_glu, ab_w_out,
           c_w_in, c_lb_raw, c_g_gain, c_w_out, ffn_w1, ffn_w3, ffn_w2):
    bsz, seq, d = x.shape
    t = bsz * seq
    depth = norm_g.shape[0]
    x2d = x.reshape(t, d)
    lbs = jnp.cumsum(jax.nn.softmax(c_lb_raw.astype(F32), axis=0), axis=0)
    for layer in range(depth):
        j = layer // 2
        if layer % 2 == 0:
            m_width = ab_conv_w.shape[2]
            s_width = ab_d.shape[1]
            x3d = x2d.reshape(bsz, seq, d)
            out_m = _mlstm(x3d, norm_g[layer, 0], ab_w_in[j][:, :2 * m_width], ab_conv_w[j], ab_conv_b[j], ab_wq[j],
                           ab_wk[j], ab_wv[j], ab_w_if[j], ab_b_if[j], ab_mh_gain[j], ab_skip[j], M_HEADS)
            out_s = _s5(x3d, norm_g[layer, 0], ab_w_in[j][:, 2 * m_width:], ab_a_re[j], ab_a_im[j], ab_log_dt[j],
                        ab_b_re[j], ab_b_im[j], ab_c_re[j], ab_c_im[j], ab_d[j], ab_w_glu[j], ab_b_glu[j])
            mixes = [out_m.reshape(t, m_width), out_s.reshape(t, s_width)]
            w_out = ab_w_out[j]
        else:
            proj = _hgrn2_proj(x2d, norm_g[layer, 0], c_w_in[j].astype(BF16), c_g_gain[j], ROW_TILE_PROJ)
            lb = lbs[layer] - lbs[0]
            mixes = [_hgrn2(proj.reshape(bsz, seq, -1), lb).reshape(t, -1)]
            w_out = c_w_out[j]
        x2d = _block_tail(x2d, mixes, w_out, norm_g[layer, 1], norm_g[layer, 2], norm_g[layer, 3],
                          ffn_w1[layer], ffn_w3[layer], ffn_w2[layer])
    return x2d.reshape(bsz, seq, d)
```

```python
import functools
import math

import jax
import jax.numpy as jnp
from jax import lax
from jax.experimental import pallas as pl
from jax.experimental.pallas import tpu as pltpu

F32 = jnp.float32
BF16 = jnp.bfloat16
EPS = 1e-6

V7X_VMEM_LIMIT_BYTES = 56 * 1024 * 1024
SUBLANES = 8
LANES = 128
MXU_TILE = 256

M_HEADS = 4
M_CONV = 4
QKV_BLOCK = 4
S_GROUP = 16
S_STATE = 64
H_EXPAND = 128

ROW_TILE_PROJ = 1024
ROW_TILE_TAIL = 1024
MLSTM_CHUNK = 256
S5_TILE = 256
S5_TILES_PER_PASS = 8
HGRN_TILE = 512
HGRN_CHUNK_UNROLL = 2
TAIL_ROW_GROUPS = 4
HGRN_CHUNK = 64


def _dot(a, b):
    return jnp.dot(a, b, preferred_element_type=F32)


def _dot_nt(a, b):
    return lax.dot_general(a, b, (((1,), (1,)), ((), ())), preferred_element_type=F32)


def _dot_tn(a, b):
    return lax.dot_general(a, b, (((0,), (0,)), ((), ())), preferred_element_type=F32)


def _sigmoid(x):
    return 1.0 / (1.0 + jnp.exp(-x))


def _silu(x):
    return x * _sigmoid(x)


def _log_sigmoid(x):
    return jnp.minimum(x, 0.0) - jnp.log(1.0 + jnp.exp(-jnp.abs(x)))


def _gelu_tanh(x):
    return 0.5 * x * (1.0 + jnp.tanh(math.sqrt(2.0 / math.pi) * (x + 0.044715 * (x * x * x))))


def _rms(x, g):
    return x * lax.rsqrt(jnp.mean(x * x, axis=-1, keepdims=True) + EPS) * g


def _prefix_rows(x, op, identity):
    rows, n = x.shape
    row = lax.broadcasted_iota(jnp.int32, (SUBLANES, n), 0)
    outs = []
    carry = None
    for r in range(rows // SUBLANES):
        t = x[r * SUBLANES:(r + 1) * SUBLANES, :]
        for k in (1, 2, 4):
            t = op(t, jnp.where(row >= k, pltpu.roll(t, k, axis=0), identity))
        if carry is not None:
            t = op(t, carry)
        carry = t[SUBLANES - 1:SUBLANES, :]
        outs.append(t)
    return jnp.concatenate(outs, axis=0)


def _cumsum_rows(x):
    return _prefix_rows(x, jnp.add, 0.0)


def _cummax_rows(x):
    return _prefix_rows(x, jnp.maximum, -jnp.inf)


def _params(n_axes):
    return pltpu.CompilerParams(dimension_semantics=("arbitrary",) * n_axes,
                                vmem_limit_bytes=V7X_VMEM_LIMIT_BYTES)


def _const_spec(shape):
    nd = len(shape)
    return pl.BlockSpec(shape, lambda *_: (0,) * nd, pipeline_mode=pl.Buffered(1))


def _hgrn2_proj_kernel(x_ref, g_ref, w_ref, gain_ref, o_ref):
    width = gain_ref.shape[1]
    h = _rms(x_ref[...], g_ref[...]).astype(BF16)
    o_ref[:, 0:width] = _silu(_dot(h, w_ref[:, 0:width]))
    o_ref[:, width:3 * width] = _dot(h, w_ref[:, width:3 * width])
    o_ref[:, 3 * width:4 * width] = gain_ref[...] * _silu(_dot(h, w_ref[:, 3 * width:4 * width]))


def _hgrn2_proj(x2d, g, w, gain, tm):
    t, d = x2d.shape
    n = w.shape[1]
    width = n // 4
    return pl.pallas_call(
        _hgrn2_proj_kernel,
        grid=(t // tm,),
        in_specs=[pl.BlockSpec((tm, d), lambda i: (i, 0)), _const_spec((1, d)), _const_spec((d, n)),
                  _const_spec((1, width))],
        out_specs=pl.BlockSpec((tm, n), lambda i: (i, 0)),
        out_shape=jax.ShapeDtypeStruct((t, n), F32),
        compiler_params=_params(1),
        name="hgrn2_proj",
    )(x2d, g.reshape(1, d), w, gain.reshape(1, width))


def _mlstm_kernel(x_ref, ng_ref, win_ref, convw_ref, convb_ref, wq_ref, wk_ref, wv_ref, wif_ref, bif_ref, gain_ref,
                  skip_ref, o_ref, xbuf, c_s, m_s, *, heads):
    n_rows, chunk, width = o_ref.shape
    hd = width // heads
    taps = convw_ref.shape[0]
    rows = range(n_rows)

    @pl.when(pl.program_id(0) == 0)
    def _init():
        xbuf[:, 0:SUBLANES, :] = jnp.zeros((n_rows, SUBLANES, width), F32)
        c_s[...] = jnp.zeros_like(c_s)
        m_s[...] = jnp.zeros_like(m_s)

    proj = [_dot(_rms(x_ref[r], ng_ref[...]).astype(BF16), win_ref[...]) for r in rows]
    xm = [p[:, 0:width] for p in proj]
    zm = [p[:, width:2 * width] for p in proj]

    xc = []
    for r in rows:
        xbuf[r, SUBLANES:SUBLANES + chunk, :] = xm[r]
        conv = convb_ref[...] + convw_ref[taps - 1:taps, :] * xm[r]
        for k in range(taps - 1):
            conv = conv + convw_ref[k:k + 1, :] * xbuf[r, pl.ds(SUBLANES - (taps - 1) + k, chunk), :]
        xbuf[r, 0:SUBLANES, :] = xm[r][chunk - SUBLANES:chunk, :]
        xc.append(_silu(conv))

    q_b, k, k_b, v_b, gates = [], [], [], [], []
    for r in rows:
        xc_b = xc[r].astype(BF16)
        q_b.append(_dot(xc_b, wq_ref[...]).astype(BF16))
        k.append(_dot(xc_b, wk_ref[...]) * (hd ** -0.5))
        k_b.append(k[r].astype(BF16))
        v_b.append(_dot(xm[r].astype(BF16), wv_ref[...]).astype(BF16))
        gates.append(_dot(q_b[r], wif_ref[0:width, :]) + _dot(k_b[r], wif_ref[width:2 * width, :])
                     + _dot(v_b[r], wif_ref[2 * width:3 * width, :]) + bif_ref[...])

    m_prev, row_term, e_neg_m, k_scale, s_old, s_new, g_rows = [], [], [], [], [], [], []
    for r in rows:
        log_i = gates[r][:, 0:LANES]
        b = _cumsum_rows(_log_sigmoid(gates[r][:, LANES:2 * LANES]))
        g = log_i - b
        g_max = _cummax_rows(g)
        m_prev.append(m_s[r, 0:1, :])
        m_t = jnp.maximum(b + m_prev[r], b + g_max)
        row_term.append(b - m_t)
        e_neg_m.append(jnp.exp(-m_t))
        b_last = b[chunk - 1:chunk, :]
        a_max = b_last + g_max[chunk - 1:chunk, :]
        k_scale.append(jnp.exp(g - g_max[chunk - 1:chunk, :]))
        m_new = jnp.maximum(b_last + m_prev[r], a_max)
        s_old.append(jnp.exp(b_last + m_prev[r] - m_new))
        s_new.append(jnp.exp(a_max - m_new))
        m_s[r] = jnp.broadcast_to(m_new, m_s.shape[1:])
        g_rows.append(g.T)

    causal = _causal_mask(chunk)
    ones_b = jnp.ones((chunk, hd), BF16)

    units = [(h, r) for h in range(heads) for r in rows]
    cols = [slice(h * hd, (h + 1) * hd) for h, _ in units]

    v_ones = [jnp.concatenate([v_b[r][:, sl], ones_b], axis=1) for (h, r), sl in zip(units, cols)]
    qk = [_dot_nt(q_b[r][:, sl], k_b[r][:, sl]) for (h, r), sl in zip(units, cols)]
    inter = [_dot(q_b[r][:, sl], c_s[r, h].astype(BF16)) for (h, r), sl in zip(units, cols)]

    scores, s_inter = [], []
    for u, (h, r) in enumerate(units):
        row_b = jnp.broadcast_to(row_term[r][:, h:h + 1], (chunk, hd))
        row_w = jnp.concatenate([row_b] * (chunk // hd), axis=1)
        w_intra = jnp.exp(jnp.where(causal, row_w + g_rows[r][h:h + 1, :], -jnp.inf))
        s_inter.append(jnp.exp(row_b + m_prev[r][:, h:h + 1]))
        scores.append((qk[u] * w_intra).astype(BF16))

    tot = [_dot(scores[u], v_ones[u]) + jnp.concatenate([s_inter[u], s_inter[u]], axis=1) * inter[u]
           for u in range(len(units))]

    for u, ((h, r), sl) in enumerate(zip(units, cols)):
        kw = (k[r][:, sl] * jnp.broadcast_to(k_scale[r][:, h:h + 1], (chunk, hd))).astype(BF16)
        c_s[r, h] = s_old[r][:, h:h + 1] * c_s[r, h] + s_new[r][:, h:h + 1] * _dot_tn(kw, v_ones[u])

    for u, ((h, r), sl) in enumerate(zip(units, cols)):
        den_floor = jnp.broadcast_to(e_neg_m[r][:, h:h + 1], (chunk, hd))
        hh = tot[u][:, 0:hd] / jnp.maximum(jnp.abs(tot[u][:, hd:2 * hd]), den_floor)
        mu = jnp.mean(hh, axis=-1, keepdims=True)
        dev = hh - mu
        var = jnp.mean(dev * dev, axis=-1, keepdims=True)
        hn = dev * lax.rsqrt(var + EPS) * gain_ref[:, sl]
        o_ref[r, :, sl] = ((hn + skip_ref[:, sl] * xc[r][:, sl]) * _silu(zm[r][:, sl])).astype(o_ref.dtype)


def _mlstm(x, norm_gain, w_in, conv_w, conv_b, wq, wk, wv, w_if, b_if, gain, skip, heads):
    b, l, d = x.shape
    width = conv_w.shape[1]
    hd = width // heads
    chunk = MLSTM_CHUNK
    def block_diag(w):
        nb, d, e = w.shape
        tiled = jnp.tile(w.reshape(nb * d, e), (1, nb))
        same_block = (jnp.arange(nb * d) // d)[:, None] == (jnp.arange(nb * e) // e)[None, :]
        return jnp.where(same_block, tiled, 0.0).astype(BF16)

    wif_pad = (jnp.zeros((w_if.shape[0], 2 * LANES), F32).at[:, :heads].set(w_if[:, :heads])
               .at[:, LANES:LANES + heads].set(w_if[:, heads:])).astype(BF16)
    bif_pad = (jnp.zeros((1, 2 * LANES), F32).at[0, :heads].set(b_if[:heads])
               .at[0, LANES:LANES + heads].set(b_if[heads:]))
    kern = functools.partial(_mlstm_kernel, heads=heads)
    return pl.pallas_call(
        kern,
        grid=(l // chunk,),
        in_specs=[pl.BlockSpec((b, chunk, d), lambda i: (0, i, 0)), _const_spec((1, d)), _const_spec(w_in.shape),
                  _const_spec(conv_w.shape), _const_spec((1, width)),
                  _const_spec((width, width)), _const_spec((width, width)), _const_spec((width, width)),
                  _const_spec((3 * width, 2 * LANES)), _const_spec((1, 2 * LANES)),
                  _const_spec((1, width)), _const_spec((1, width))],
        out_specs=pl.BlockSpec((b, chunk, width), lambda i: (0, i, 0)),
        out_shape=jax.ShapeDtypeStruct((b, l, width), BF16),
        scratch_shapes=[pltpu.VMEM((b, chunk + SUBLANES, width), F32),
                        pltpu.VMEM((b, heads, hd, 2 * hd), F32),
                        pltpu.VMEM((b, SUBLANES, LANES), F32)],
        compiler_params=_params(1),
        name="mlstm",
    )(x, norm_gain.reshape(1, d), w_in.astype(BF16), conv_w, conv_b.reshape(1, width),
      block_diag(wq), block_diag(wk), block_diag(wv),
      wif_pad, bif_pad, gain.reshape(1, width), skip.reshape(1, width))


def _s5_kernel(x_ref, ng_ref, win_ref, pm_ref, pmt_ref, wb_ref, cst_ref, wc_ref, d_ref, wglu_ref, bglu_ref,
               o_ref, xs, carry):
    n_rows, tm, width = o_ref.shape
    seg = tm // SUBLANES
    n_tiles = wb_ref.shape[0]
    kt = wb_ref.shape[1]
    tiles_per_k = n_tiles // (width // kt)
    xw = 2 * LANES
    rows = range(n_rows)

    @pl.when(pl.program_id(0) == 0)
    def _init():
        carry[...] = jnp.zeros_like(carry)

    u = []
    for r in rows:
        h_b = _rms(x_ref[r], ng_ref[...]).astype(BF16)
        u.append(_dot(_dot(pm_ref[...], h_b).astype(BF16), win_ref[...]))
    for r in rows:
        u_b = u[r].astype(BF16)
        for j in range(n_tiles):
            kk = j // tiles_per_k
            xs[r, :, j * xw:(j + 1) * xw] = _dot(u_b[:, kk * kt:(kk + 1) * kt], wb_ref[j])

    row_id = lax.broadcasted_iota(jnp.int32, (SUBLANES, LANES), 0)

    def recur(r, t, state, tiles, store):
        r8 = t * SUBLANES
        new = []
        for idx, j in enumerate(tiles):
            ar, ai = cst_ref[j, 0], cst_ref[j, 1]
            xr, xi = state[2 * idx], state[2 * idx + 1]
            br = xs[r, pl.ds(r8, SUBLANES), j * xw:j * xw + LANES]
            bi = xs[r, pl.ds(r8, SUBLANES), j * xw + LANES:(j + 1) * xw]
            xr, xi = (ar * xr - ai * xi) + br, (ar * xi + ai * xr) + bi
            if store:
                xs[r, pl.ds(r8, SUBLANES), j * xw:j * xw + LANES] = xr
                xs[r, pl.ds(r8, SUBLANES), j * xw + LANES:(j + 1) * xw] = xi
            new += [xr, xi]
        return new

    n_out = wc_ref.shape[0]
    kc = wc_ref.shape[1]
    for r in rows:
        for j0 in range(0, n_tiles, S5_TILES_PER_PASS):
            tiles = tuple(range(j0, min(j0 + S5_TILES_PER_PASS, n_tiles)))
            state = [jnp.zeros((SUBLANES, LANES), F32) for _ in range(2 * len(tiles))]
            for t in range(seg):
                state = recur(r, t, state, tiles, False)

            starts = []
            for idx, j in enumerate(tiles):
                er, ei = state[2 * idx], state[2 * idx + 1]
                for lvl, k in enumerate((1, 2, 4)):
                    mr, mi = cst_ref[j, 2 + 2 * lvl], cst_ref[j, 3 + 2 * lvl]
                    sr, si = pltpu.roll(er, k, axis=0), pltpu.roll(ei, k, axis=0)
                    er, ei = er + (mr * sr - mi * si), ei + (mr * si + mi * sr)
                pr, pi = cst_ref[j, 8], cst_ref[j, 9]
                cr, ci = carry[r, j, 0], carry[r, j, 1]
                er, ei = er + (pr * cr - pi * ci), ei + (pr * ci + pi * cr)
                carry[r, j, 0] = jnp.broadcast_to(er[SUBLANES - 1:SUBLANES, :], (SUBLANES, LANES))
                carry[r, j, 1] = jnp.broadcast_to(ei[SUBLANES - 1:SUBLANES, :], (SUBLANES, LANES))
                starts.append(jnp.where(row_id == 0, cr, pltpu.roll(er, 1, axis=0)))
                starts.append(jnp.where(row_id == 0, ci, pltpu.roll(ei, 1, axis=0)))
            state = starts
            for t in range(seg):
                state = recur(r, t, state, tiles, True)

        y = jnp.concatenate([_dot(xs[r, :, h * kc:(h + 1) * kc].astype(BF16), wc_ref[h]) for h in range(n_out)],
                            axis=1)
        y = _gelu_tanh(y + d_ref[...] * u[r])
        gate = _sigmoid(_dot(y.astype(BF16), wglu_ref[...]) + bglu_ref[...])
        o_ref[r] = _dot(pmt_ref[...], (y * gate).astype(BF16)).astype(o_ref.dtype)


def _s5_weights(a_re, a_im, log_dt, b_re, b_im, c_re, c_im, seg):
    groups, n_state, p = b_re.shape
    width = groups * p
    gpt = LANES // n_state
    n_tiles = groups // gpt
    dt = jnp.exp(log_dt)[:, None]
    mag = jnp.exp(a_re * dt)
    ab_re = mag * jnp.cos(a_im * dt)
    ab_im = mag * jnp.sin(a_im * dt)
    inv = 1.0 / (a_re * a_re + a_im * a_im)
    g_re = ((ab_re - 1.0) * a_re + ab_im * a_im) * inv
    g_im = (ab_im * a_re - (ab_re - 1.0) * a_im) * inv
    bb_re = g_re[..., None] * b_re - g_im[..., None] * b_im
    bb_im = g_re[..., None] * b_im + g_im[..., None] * b_re
    kt = min(MXU_TILE, width)
    n_k = width // kt
    tiles_per_k = n_tiles // n_k
    xw = 2 * LANES

    bb = jnp.stack([bb_re, bb_im], axis=0)
    r_in = jnp.transpose(bb, (1, 3, 0, 2))
    r_in = jnp.broadcast_to(r_in[:, :, :, None, :], (groups, p, 2, gpt, n_state)).reshape(width, xw)
    row_g = jnp.arange(width) // p
    col_gg = (jnp.arange(xw) % LANES) // n_state
    dense_in = jnp.where((row_g % gpt)[:, None] == col_gg[None, :], r_in, 0.0).reshape(n_k, kt, xw)
    tile_id = jnp.arange(n_tiles)
    row_tile = (row_g // gpt).reshape(n_k, kt)[tile_id // tiles_per_k]
    wb = jnp.where(row_tile[:, :, None] == tile_id[:, None, None], dense_in[tile_id // tiles_per_k], 0.0)

    cc = jnp.stack([c_re, -c_im], axis=0)
    r_out = jnp.transpose(cc, (1, 0, 3, 2)).reshape(n_tiles, gpt, 2, n_state, p)
    r_out = jnp.transpose(r_out, (0, 2, 1, 3, 4)).reshape(n_tiles * xw, p)
    r_out = jnp.tile(r_out, (1, kt // p))
    out_row_g = (jnp.arange(n_tiles * xw) // xw) * gpt + (jnp.arange(n_tiles * xw) % LANES) // n_state
    out_col_g = jnp.arange(kt) // p
    wc = jnp.where((out_row_g % (kt // p))[:, None] == out_col_g[None, :], r_out, 0.0)
    wc = wc.reshape(n_k, tiles_per_k * xw, kt)

    a1 = (ab_re.reshape(n_tiles, LANES), ab_im.reshape(n_tiles, LANES))

    def cmul(x, y):
        return x[0] * y[0] - x[1] * y[1], x[0] * y[1] + x[1] * y[0]

    a_seg = a1
    assert seg & (seg - 1) == 0, "segment length must be a power of two"
    for _ in range(seg.bit_length() - 1):
        a_seg = cmul(a_seg, a_seg)
    pows = [a_seg]
    for _ in range(SUBLANES - 1):
        pows.append(cmul(pows[-1], a_seg))
    row = jnp.arange(SUBLANES)[None, :, None]
    planes = [jnp.broadcast_to(a1[c][:, None, :], (n_tiles, SUBLANES, LANES)) for c in range(2)]
    for k in (1, 2, 4):
        for c in range(2):
            planes.append(jnp.where(row >= k, pows[k - 1][c][:, None, :], 0.0))
    for c in range(2):
        planes.append(jnp.stack([pw[c] for pw in pows], axis=1))
    cst = jnp.stack(planes, axis=1).astype(F32)
    return wb.astype(BF16), cst, wc.astype(BF16)


def _s5(x, norm_gain, w_in, a_re, a_im, log_dt, b_re, b_im, c_re, c_im, d_skip, w_glu, b_glu):
    b, l, d = x.shape
    width = d_skip.shape[0]
    tm = S5_TILE
    seg = tm // SUBLANES
    wb, cst, wc = _s5_weights(a_re, a_im, log_dt, b_re, b_im, c_re, c_im, seg)
    n_tiles = wb.shape[0]
    rp = jnp.arange(tm)
    pm = ((rp % SUBLANES) * seg + rp // SUBLANES)[:, None] == rp[None, :]
    return pl.pallas_call(
        _s5_kernel,
        grid=(l // tm,),
        in_specs=[pl.BlockSpec((b, tm, d), lambda i: (0, i, 0)), _const_spec((1, d)), _const_spec(w_in.shape),
                  _const_spec((tm, tm)), _const_spec((tm, tm)),
                  _const_spec(wb.shape), _const_spec(cst.shape), _const_spec(wc.shape),
                  _const_spec((1, width)), _const_spec((width, width)), _const_spec((1, width))],
        out_specs=pl.BlockSpec((b, tm, width), lambda i: (0, i, 0)),
        out_shape=jax.ShapeDtypeStruct((b, l, width), BF16),
        scratch_shapes=[pltpu.VMEM((b, tm, n_tiles * 2 * LANES), F32),
                        pltpu.VMEM((b, n_tiles, 2, SUBLANES, LANES), F32)],
        compiler_params=_params(1),
        name="s5",
    )(x, norm_gain.reshape(1, d), w_in.astype(BF16), pm.astype(BF16), pm.T.astype(BF16), wb, cst, wc,
      d_skip.reshape(1, width), w_glu.astype(BF16),
      b_glu.reshape(1, width))


def _hgrn2_chunk(p_ref, r0, lb, st, causal, o_ref, *, heads, chunk):
    n_rows, _, width = o_ref.shape
    e = width // heads
    rows = pl.ds(r0, chunk)
    units = [(h, r) for h in range(heads) for r in range(n_rows)]

    q_inter, k_state, q_intra, v_b, gate, decay = [], [], [], [], [], []
    for h, r in units:
        sl = slice(h * e, (h + 1) * e)
        q = p_ref[r, rows, h * e:(h + 1) * e]
        f = p_ref[r, rows, width + h * e:width + (h + 1) * e]
        lb_h = lb[:, sl]
        fg = lb_h + (1.0 - lb_h) * _sigmoid(f)
        b = _cumsum_rows(jnp.log(fg))
        b_last = b[chunk - 1:chunk, :]
        q_inter.append((q * jnp.exp(b)).astype(BF16))
        to_end = jnp.exp(b_last - b)
        k_state.append(((1.0 - fg) * to_end).astype(BF16))
        q_intra.append((q * (1.0 / to_end)).astype(BF16))
        v_b.append(p_ref[r, rows, 2 * width + h * e:2 * width + (h + 1) * e].astype(BF16))
        gate.append(p_ref[r, rows, 3 * width + h * e:3 * width + (h + 1) * e])
        decay.append(jnp.exp(b_last))

    scores = [_dot_nt(q_intra[u], k_state[u]) for u in range(len(units))]
    scores = [jnp.where(causal, sc, 0.0).astype(BF16) for sc in scores]
    outs = []
    for u, (h, r) in enumerate(units):
        s_t = st[r, h]
        outs.append(_dot(scores[u], v_b[u]) + _dot_nt(q_inter[u], s_t.astype(BF16)))
        st[r, h] = s_t * decay[u] + _dot_tn(v_b[u], k_state[u])
    for u, (h, r) in enumerate(units):
        o = outs[u]
        o = o * lax.rsqrt(jnp.mean(o * o, axis=-1, keepdims=True) + EPS)
        o_ref[r, rows, h * e:(h + 1) * e] = (o * gate[u]).astype(o_ref.dtype)


def _causal_mask(chunk):
    ti = lax.broadcasted_iota(jnp.int32, (chunk, chunk), 0)
    si = lax.broadcasted_iota(jnp.int32, (chunk, chunk), 1)
    return si <= ti


def _tail_math(xs, mix_groups, wo_ref, g1_ref, g2_ref, g3_ref, w1_ref, w3_ref, w2_ref):
    mps = []
    for mix_refs in mix_groups:
        mp = None
        row0 = 0
        for m_ref in mix_refs:
            part = _dot(m_ref[...], wo_ref[row0:row0 + m_ref.shape[1], :])
            mp = part if mp is None else mp + part
            row0 += m_ref.shape[1]
        mps.append(mp)
    x1s = [x + _rms(mp, g1_ref[...]) for x, mp in zip(xs, mps)]
    hs = [_rms(x1, g2_ref[...]).astype(BF16) for x1 in x1s]
    ys = []
    for h in hs:
        z = (_silu(_dot(h, w1_ref[...])) * _dot(h, w3_ref[...])).astype(BF16)
        ys.append(_dot(z, w2_ref[...]))
    return [x1 + _rms(y, g3_ref[...]) for x1, y in zip(x1s, ys)]


def _hgrn2_kernel(p_ref, lb_ref, o_ref, st, *, heads, chunk):
    tm = o_ref.shape[1]

    @pl.when(pl.program_id(0) == 0)
    def _init():
        st[...] = jnp.zeros_like(st)

    lb = lb_ref[...]
    causal = _causal_mask(chunk)

    def body(c, _):
        _hgrn2_chunk(p_ref, pl.multiple_of(c * chunk, chunk), lb, st, causal, o_ref, heads=heads, chunk=chunk)
        return 0

    lax.fori_loop(0, tm // chunk, body, 0, unroll=HGRN_CHUNK_UNROLL)


def _hgrn2(proj, lb):
    b, l, w4 = proj.shape
    width = w4 // 4
    heads = width // H_EXPAND
    tm = HGRN_TILE
    kern = functools.partial(_hgrn2_kernel, heads=heads, chunk=HGRN_CHUNK)
    return pl.pallas_call(
        kern,
        grid=(l // tm,),
        in_specs=[pl.BlockSpec((b, tm, w4), lambda i: (0, i, 0)), _const_spec((1, width))],
        out_specs=pl.BlockSpec((b, tm, width), lambda i: (0, i, 0)),
        out_shape=jax.ShapeDtypeStruct((b, l, width), BF16),
        scratch_shapes=[pltpu.VMEM((b, heads, H_EXPAND, H_EXPAND), F32)],
        compiler_params=_params(1),
        name="hgrn2",
    )(proj, lb.reshape(1, width))


def _tail_kernel(*refs, n_mix):
    x_ref = refs[0]
    mix_refs = refs[1:1 + n_mix]
    wo_ref, g1_ref, g2_ref, g3_ref, w1_ref, w3_ref, w2_ref, o_ref = refs[1 + n_mix:]
    rows = o_ref.shape[0] // TAIL_ROW_GROUPS
    groups = [slice(r * rows, (r + 1) * rows) for r in range(TAIL_ROW_GROUPS)]
    outs = _tail_math([x_ref[sl, :] for sl in groups], [[m.at[sl, :] for m in mix_refs] for sl in groups],
                      wo_ref, g1_ref, g2_ref, g3_ref, w1_ref, w3_ref, w2_ref)
    for sl, out in zip(groups, outs):
        o_ref[sl, :] = out


def _block_tail(x2d, mixes, w_out, g1, g2, g3, w1, w3, w2, layer):
    t, d = x2d.shape
    f = w1.shape[2]
    tm = ROW_TILE_TAIL
    n_mix = len(mixes)
    row = lambda i: (i, 0)
    layer_spec = lambda r, c: pl.BlockSpec((None, r, c), lambda i: (layer, 0, 0), pipeline_mode=pl.Buffered(1))
    in_specs = ([pl.BlockSpec((tm, d), row)]
                + [pl.BlockSpec((tm, m.shape[1]), row) for m in mixes]
                + [_const_spec(w_out.shape)]
                + [_const_spec((1, d))] * 3
                + [layer_spec(d, f), layer_spec(d, f), layer_spec(f, d)])
    return pl.pallas_call(
        functools.partial(_tail_kernel, n_mix=n_mix),
        grid=(t // tm,),
        in_specs=in_specs,
        out_specs=pl.BlockSpec((tm, d), row),
        out_shape=jax.ShapeDtypeStruct((t, d), F32),
        compiler_params=_params(1),
        name="outproj_swiglu",
    )(x2d, *mixes, w_out.astype(BF16), g1.reshape(1, d), g2.reshape(1, d), g3.reshape(1, d), w1, w3, w2)


def kernel(x, norm_g, ab_w_in, ab_conv_w, ab_conv_b, ab_wq, ab_wk, ab_wv, ab_w_if, ab_b_if, ab_mh_gain, ab_skip,
           ab_a_re, ab_a_im, ab_log_dt, ab_b_re, ab_b_im, ab_c_re, ab_c_im, ab_d, ab_w_glu, ab_b_glu, ab_w_out,
           c_w_in, c_lb_raw, c_g_gain, c_w_out, ffn_w1, ffn_w3, ffn_w2):
    bsz, seq, d = x.shape
    t = bsz * seq
    depth = norm_g.shape[0]
    x2d = x.reshape(t, d)
    lbs = jnp.cumsum(jax.nn.softmax(c_lb_raw.astype(F32), axis=0), axis=0)
    w1_b, w3_b, w2_b = ffn_w1.astype(BF16), ffn_w3.astype(BF16), ffn_w2.astype(BF16)
    for layer in range(depth):
        j = layer // 2
        if layer % 2 == 0:
            m_width = ab_conv_w.shape[2]
            s_width = ab_d.shape[1]
            x3d = x2d.reshape(bsz, seq, d)
            out_m = _mlstm(x3d, norm_g[layer, 0], ab_w_in[j][:, :2 * m_width], ab_conv_w[j], ab_conv_b[j], ab_wq[j],
                           ab_wk[j], ab_wv[j], ab_w_if[j], ab_b_if[j], ab_mh_gain[j], ab_skip[j], M_HEADS)
            out_s = _s5(x3d, norm_g[layer, 0], ab_w_in[j][:, 2 * m_width:], ab_a_re[j], ab_a_im[j], ab_log_dt[j],
                        ab_b_re[j], ab_b_im[j], ab_c_re[j], ab_c_im[j], ab_d[j], ab_w_glu[j], ab_b_glu[j])
            mixes = [out_m.reshape(t, m_width), out_s.reshape(t, s_width)]
            w_out = ab_w_out[j]
        else:
            proj = _hgrn2_proj(x2d, norm_g[layer, 0], c_w_in[j].astype(BF16), c_g_gain[j], ROW_TILE_PROJ)
            lb = lbs[layer] - lbs[0]
            mixes = [_hgrn2(proj.reshape(bsz, seq, -1), lb).reshape(t, -1)]
            w_out = c_w_out[j]
        x2d = _block_tail(x2d, mixes, w_out, norm_g[layer, 1], norm_g[layer, 2], norm_g[layer, 3],
                          w1_b, w3_b, w2_b, layer)
    return x2d.reshape(bsz, seq, d)
```

```python
import functools
import math

import jax
import jax.numpy as jnp
from jax import lax
from jax.experimental import pallas as pl
from jax.experimental.pallas import tpu as pltpu

F32 = jnp.float32
BF16 = jnp.bfloat16
EPS = 1e-6

V7X_VMEM_LIMIT_BYTES = 56 * 1024 * 1024
SUBLANES = 8
LANES = 128
MXU_TILE = 256

M_HEADS = 4
M_CONV = 4
QKV_BLOCK = 4
S_GROUP = 16
S_STATE = 64
H_EXPAND = 128

ROW_TILE_PROJ = 1024
ROW_TILE_TAIL = 1024
MLSTM_CHUNK = 256
S5_TILE = 256
S5_TILES_PER_PASS = 8
HGRN_TILE = 512
HGRN_CHUNK_UNROLL = 2
TAIL_ROW_GROUPS = 4
HGRN_CHUNK = 64


def _dot(a, b):
    return jnp.dot(a, b, preferred_element_type=F32)


def _dot_nt(a, b):
    return lax.dot_general(a, b, (((1,), (1,)), ((), ())), preferred_element_type=F32)


def _dot_tn(a, b):
    return lax.dot_general(a, b, (((0,), (0,)), ((), ())), preferred_element_type=F32)


def _sigmoid(x):
    return 1.0 / (1.0 + jnp.exp(-x))


def _silu(x):
    return x * _sigmoid(x)


def _log_sigmoid(x):
    return jnp.minimum(x, 0.0) - jnp.log(1.0 + jnp.exp(-jnp.abs(x)))


def _gelu_tanh(x):
    return 0.5 * x * (1.0 + jnp.tanh(math.sqrt(2.0 / math.pi) * (x + 0.044715 * (x * x * x))))


def _rms(x, g):
    return x * lax.rsqrt(jnp.mean(x * x, axis=-1, keepdims=True) + EPS) * g


def _prefix_rows(x, op, identity):
    rows, n = x.shape
    row = lax.broadcasted_iota(jnp.int32, (SUBLANES, n), 0)
    outs = []
    carry = None
    for r in range(rows // SUBLANES):
        t = x[r * SUBLANES:(r + 1) * SUBLANES, :]
        for k in (1, 2, 4):
            t = op(t, jnp.where(row >= k, pltpu.roll(t, k, axis=0), identity))
        if carry is not None:
            t = op(t, carry)
        carry = t[SUBLANES - 1:SUBLANES, :]
        outs.append(t)
    return jnp.concatenate(outs, axis=0)


def _cumsum_rows(x):
    return _prefix_rows(x, jnp.add, 0.0)


def _cummax_rows(x):
    return _prefix_rows(x, jnp.maximum, -jnp.inf)


def _params(n_axes):
    return pltpu.CompilerParams(dimension_semantics=("arbitrary",) * n_axes,
                                vmem_limit_bytes=V7X_VMEM_LIMIT_BYTES)


def _const_spec(shape):
    nd = len(shape)
    return pl.BlockSpec(shape, lambda *_: (0,) * nd, pipeline_mode=pl.Buffered(1))


def _hgrn2_proj_kernel(x_ref, g_ref, w_ref, gain_ref, o_ref):
    width = gain_ref.shape[1]
    rows = o_ref.shape[0] // TAIL_ROW_GROUPS
    groups = [slice(r * rows, (r + 1) * rows) for r in range(TAIL_ROW_GROUPS)]
    hs = [_rms(x_ref[sl, :], g_ref[...]).astype(BF16) for sl in groups]
    for sl, h in zip(groups, hs):
        o_ref[sl, 0:width] = _silu(_dot(h, w_ref[:, 0:width]))
        o_ref[sl, width:3 * width] = _dot(h, w_ref[:, width:3 * width])
        o_ref[sl, 3 * width:4 * width] = gain_ref[...] * _silu(_dot(h, w_ref[:, 3 * width:4 * width]))


def _hgrn2_proj(x2d, g, w, gain, tm):
    t, d = x2d.shape
    n = w.shape[1]
    width = n // 4
    return pl.pallas_call(
        _hgrn2_proj_kernel,
        grid=(t // tm,),
        in_specs=[pl.BlockSpec((tm, d), lambda i: (i, 0)), _const_spec((1, d)), _const_spec((d, n)),
                  _const_spec((1, width))],
        out_specs=pl.BlockSpec((tm, n), lambda i: (i, 0)),
        out_shape=jax.ShapeDtypeStruct((t, n), F32),
        compiler_params=_params(1),
        name="hgrn2_proj",
    )(x2d, g.reshape(1, d), w, gain.reshape(1, width))


def _mlstm_kernel(x_ref, ng_ref, win_ref, convw_ref, convb_ref, wq_ref, wk_ref, wv_ref, wif_ref, bif_ref, gain_ref,
                  skip_ref, o_ref, xbuf, c_s, m_s, *, heads):
    n_rows, chunk, width = o_ref.shape
    hd = width // heads
    taps = convw_ref.shape[0]
    rows = range(n_rows)

    @pl.when(pl.program_id(0) == 0)
    def _init():
        xbuf[:, 0:SUBLANES, :] = jnp.zeros((n_rows, SUBLANES, width), F32)
        c_s[...] = jnp.zeros_like(c_s)
        m_s[...] = jnp.zeros_like(m_s)

    proj = [_dot(_rms(x_ref[r], ng_ref[...]).astype(BF16), win_ref[:, 0:2 * width]) for r in rows]
    xm = [p[:, 0:width] for p in proj]
    zm = [p[:, width:2 * width] for p in proj]

    xc = []
    for r in rows:
        xbuf[r, SUBLANES:SUBLANES + chunk, :] = xm[r]
        conv = convb_ref[...] + convw_ref[taps - 1:taps, :] * xm[r]
        for k in range(taps - 1):
            conv = conv + convw_ref[k:k + 1, :] * xbuf[r, pl.ds(SUBLANES - (taps - 1) + k, chunk), :]
        xbuf[r, 0:SUBLANES, :] = xm[r][chunk - SUBLANES:chunk, :]
        xc.append(_silu(conv))

    q_b, k, k_b, v_b, gates = [], [], [], [], []
    for r in rows:
        xc_b = xc[r].astype(BF16)
        q_b.append(_dot(xc_b, wq_ref[...]).astype(BF16))
        k.append(_dot(xc_b, wk_ref[...]) * (hd ** -0.5))
        k_b.append(k[r].astype(BF16))
        v_b.append(_dot(xm[r].astype(BF16), wv_ref[...]).astype(BF16))
        gates.append(_dot(q_b[r], wif_ref[0:width, :]) + _dot(k_b[r], wif_ref[width:2 * width, :])
                     + _dot(v_b[r], wif_ref[2 * width:3 * width, :]) + bif_ref[...])

    m_prev, row_term, e_neg_m, k_scale, s_old, s_new, g_rows = [], [], [], [], [], [], []
    for r in rows:
        log_i = gates[r][:, 0:LANES]
        b = _cumsum_rows(_log_sigmoid(gates[r][:, LANES:2 * LANES]))
        g = log_i - b
        g_max = _cummax_rows(g)
        m_prev.append(m_s[r, 0:1, :])
        m_t = jnp.maximum(b + m_prev[r], b + g_max)
        row_term.append(b - m_t)
        e_neg_m.append(jnp.exp(-m_t))
        b_last = b[chunk - 1:chunk, :]
        a_max = b_last + g_max[chunk - 1:chunk, :]
        k_scale.append(jnp.exp(g - g_max[chunk - 1:chunk, :]))
        m_new = jnp.maximum(b_last + m_prev[r], a_max)
        s_old.append(jnp.exp(b_last + m_prev[r] - m_new))
        s_new.append(jnp.exp(a_max - m_new))
        m_s[r] = jnp.broadcast_to(m_new, m_s.shape[1:])
        g_rows.append(g.T)

    causal = _causal_mask(chunk)
    ones_b = jnp.ones((chunk, hd), BF16)

    units = [(h, r) for h in range(heads) for r in rows]
    cols = [slice(h * hd, (h + 1) * hd) for h, _ in units]

    v_ones = [jnp.concatenate([v_b[r][:, sl], ones_b], axis=1) for (h, r), sl in zip(units, cols)]
    qk = [_dot_nt(q_b[r][:, sl], k_b[r][:, sl]) for (h, r), sl in zip(units, cols)]
    inter = [_dot(q_b[r][:, sl], c_s[r, h].astype(BF16)) for (h, r), sl in zip(units, cols)]

    scores, s_inter = [], []
    for u, (h, r) in enumerate(units):
        row_b = jnp.broadcast_to(row_term[r][:, h:h + 1], (chunk, hd))
        row_w = jnp.concatenate([row_b] * (chunk // hd), axis=1)
        w_intra = jnp.exp(jnp.where(causal, row_w + g_rows[r][h:h + 1, :], -jnp.inf))
        s_inter.append(jnp.exp(row_b + m_prev[r][:, h:h + 1]))
        scores.append((qk[u] * w_intra).astype(BF16))

    tot = [_dot(scores[u], v_ones[u]) + jnp.concatenate([s_inter[u], s_inter[u]], axis=1) * inter[u]
           for u in range(len(units))]

    for u, ((h, r), sl) in enumerate(zip(units, cols)):
        kw = (k[r][:, sl] * jnp.broadcast_to(k_scale[r][:, h:h + 1], (chunk, hd))).astype(BF16)
        c_s[r, h] = s_old[r][:, h:h + 1] * c_s[r, h] + s_new[r][:, h:h + 1] * _dot_tn(kw, v_ones[u])

    for u, ((h, r), sl) in enumerate(zip(units, cols)):
        den_floor = jnp.broadcast_to(e_neg_m[r][:, h:h + 1], (chunk, hd))
        hh = tot[u][:, 0:hd] / jnp.maximum(jnp.abs(tot[u][:, hd:2 * hd]), den_floor)
        mu = jnp.mean(hh, axis=-1, keepdims=True)
        dev = hh - mu
        var = jnp.mean(dev * dev, axis=-1, keepdims=True)
        hn = dev * lax.rsqrt(var + EPS) * gain_ref[:, sl]
        o_ref[r, :, sl] = ((hn + skip_ref[:, sl] * xc[r][:, sl]) * _silu(zm[r][:, sl])).astype(o_ref.dtype)


def _mlstm(x, norm_gain, w_in, conv_w, conv_b, wq, wk, wv, w_if, b_if, gain, skip, heads):
    b, l, d = x.shape
    width = conv_w.shape[1]
    hd = width // heads
    chunk = MLSTM_CHUNK
    def block_diag(w):
        nb, d, e = w.shape
        tiled = jnp.tile(w.reshape(nb * d, e), (1, nb))
        same_block = (jnp.arange(nb * d) // d)[:, None] == (jnp.arange(nb * e) // e)[None, :]
        return jnp.where(same_block, tiled, 0.0).astype(BF16)

    wif_pad = (jnp.zeros((w_if.shape[0], 2 * LANES), F32).at[:, :heads].set(w_if[:, :heads])
               .at[:, LANES:LANES + heads].set(w_if[:, heads:])).astype(BF16)
    bif_pad = (jnp.zeros((1, 2 * LANES), F32).at[0, :heads].set(b_if[:heads])
               .at[0, LANES:LANES + heads].set(b_if[heads:]))
    kern = functools.partial(_mlstm_kernel, heads=heads)
    return pl.pallas_call(
        kern,
        grid=(l // chunk,),
        in_specs=[pl.BlockSpec((b, chunk, d), lambda i: (0, i, 0)), _const_spec((1, d)), _const_spec(w_in.shape),
                  _const_spec(conv_w.shape), _const_spec((1, width)),
                  _const_spec((width, width)), _const_spec((width, width)), _const_spec((width, width)),
                  _const_spec((3 * width, 2 * LANES)), _const_spec((1, 2 * LANES)),
                  _const_spec((1, width)), _const_spec((1, width))],
        out_specs=pl.BlockSpec((b, chunk, width), lambda i: (0, i, 0)),
        out_shape=jax.ShapeDtypeStruct((b, l, width), BF16),
        scratch_shapes=[pltpu.VMEM((b, chunk + SUBLANES, width), F32),
                        pltpu.VMEM((b, heads, hd, 2 * hd), F32),
                        pltpu.VMEM((b, SUBLANES, LANES), F32)],
        compiler_params=_params(1),
        name="mlstm",
    )(x, norm_gain.reshape(1, d), w_in.astype(BF16), conv_w, conv_b.reshape(1, width),
      block_diag(wq), block_diag(wk), block_diag(wv),
      wif_pad, bif_pad, gain.reshape(1, width), skip.reshape(1, width))


def _s5_kernel(x_ref, ng_ref, win_ref, pm_ref, pmt_ref, wb_ref, cst_ref, wc_ref, d_ref, wglu_ref, bglu_ref,
               o_ref, xs, carry, *, col0):
    n_rows, tm, width = o_ref.shape
    seg = tm // SUBLANES
    n_tiles = wb_ref.shape[0]
    kt = wb_ref.shape[1]
    tiles_per_k = n_tiles // (width // kt)
    xw = 2 * LANES
    rows = range(n_rows)

    @pl.when(pl.program_id(0) == 0)
    def _init():
        carry[...] = jnp.zeros_like(carry)

    u = []
    for r in rows:
        h_b = _rms(x_ref[r], ng_ref[...]).astype(BF16)
        u.append(_dot(_dot(pm_ref[...], h_b).astype(BF16), win_ref[:, col0:col0 + width]))
    for r in rows:
        u_b = u[r].astype(BF16)
        for j in range(n_tiles):
            kk = j // tiles_per_k
            xs[r, :, j * xw:(j + 1) * xw] = _dot(u_b[:, kk * kt:(kk + 1) * kt], wb_ref[j])

    row_id = lax.broadcasted_iota(jnp.int32, (SUBLANES, LANES), 0)

    def recur(r, t, state, tiles, store):
        r8 = t * SUBLANES
        new = []
        for idx, j in enumerate(tiles):
            ar, ai = cst_ref[j, 0], cst_ref[j, 1]
            xr, xi = state[2 * idx], state[2 * idx + 1]
            br = xs[r, pl.ds(r8, SUBLANES), j * xw:j * xw + LANES]
            bi = xs[r, pl.ds(r8, SUBLANES), j * xw + LANES:(j + 1) * xw]
            xr, xi = (ar * xr - ai * xi) + br, (ar * xi + ai * xr) + bi
            if store:
                xs[r, pl.ds(r8, SUBLANES), j * xw:j * xw + LANES] = xr
                xs[r, pl.ds(r8, SUBLANES), j * xw + LANES:(j + 1) * xw] = xi
            new += [xr, xi]
        return new

    n_out = wc_ref.shape[0]
    kc = wc_ref.shape[1]
    for r in rows:
        for j0 in range(0, n_tiles, S5_TILES_PER_PASS):
            tiles = tuple(range(j0, min(j0 + S5_TILES_PER_PASS, n_tiles)))
            state = [jnp.zeros((SUBLANES, LANES), F32) for _ in range(2 * len(tiles))]
            for t in range(seg):
                state = recur(r, t, state, tiles, False)

            starts = []
            for idx, j in enumerate(tiles):
                er, ei = state[2 * idx], state[2 * idx + 1]
                for lvl, k in enumerate((1, 2, 4)):
                    mr, mi = cst_ref[j, 2 + 2 * lvl], cst_ref[j, 3 + 2 * lvl]
                    sr, si = pltpu.roll(er, k, axis=0), pltpu.roll(ei, k, axis=0)
                    er, ei = er + (mr * sr - mi * si), ei + (mr * si + mi * sr)
                pr, pi = cst_ref[j, 8], cst_ref[j, 9]
                cr, ci = carry[r, j, 0], carry[r, j, 1]
                er, ei = er + (pr * cr - pi * ci), ei + (pr * ci + pi * cr)
                carry[r, j, 0] = jnp.broadcast_to(er[SUBLANES - 1:SUBLANES, :], (SUBLANES, LANES))
                carry[r, j, 1] = jnp.broadcast_to(ei[SUBLANES - 1:SUBLANES, :], (SUBLANES, LANES))
                starts.append(jnp.where(row_id == 0, cr, pltpu.roll(er, 1, axis=0)))
                starts.append(jnp.where(row_id == 0, ci, pltpu.roll(ei, 1, axis=0)))
            state = starts
            for t in range(seg):
                state = recur(r, t, state, tiles, True)

        y = jnp.concatenate([_dot(xs[r, :, h * kc:(h + 1) * kc].astype(BF16), wc_ref[h]) for h in range(n_out)],
                            axis=1)
        y = _gelu_tanh(y + d_ref[...] * u[r])
        gate = _sigmoid(_dot(y.astype(BF16), wglu_ref[...]) + bglu_ref[...])
        o_ref[r] = _dot(pmt_ref[...], (y * gate).astype(BF16)).astype(o_ref.dtype)


def _s5_weights(a_re, a_im, log_dt, b_re, b_im, c_re, c_im, seg):
    groups, n_state, p = b_re.shape
    width = groups * p
    gpt = LANES // n_state
    n_tiles = groups // gpt
    dt = jnp.exp(log_dt)[:, None]
    mag = jnp.exp(a_re * dt)
    ab_re = mag * jnp.cos(a_im * dt)
    ab_im = mag * jnp.sin(a_im * dt)
    inv = 1.0 / (a_re * a_re + a_im * a_im)
    g_re = ((ab_re - 1.0) * a_re + ab_im * a_im) * inv
    g_im = (ab_im * a_re - (ab_re - 1.0) * a_im) * inv
    bb_re = g_re[..., None] * b_re - g_im[..., None] * b_im
    bb_im = g_re[..., None] * b_im + g_im[..., None] * b_re
    kt = min(MXU_TILE, width)
    n_k = width // kt
    tiles_per_k = n_tiles // n_k
    xw = 2 * LANES

    bb = jnp.stack([bb_re, bb_im], axis=0)
    r_in = jnp.transpose(bb, (1, 3, 0, 2))
    r_in = jnp.broadcast_to(r_in[:, :, :, None, :], (groups, p, 2, gpt, n_state)).reshape(width, xw)
    row_g = jnp.arange(width) // p
    col_gg = (jnp.arange(xw) % LANES) // n_state
    dense_in = jnp.where((row_g % gpt)[:, None] == col_gg[None, :], r_in, 0.0).reshape(n_k, kt, xw)
    tile_id = jnp.arange(n_tiles)
    row_tile = (row_g // gpt).reshape(n_k, kt)[tile_id // tiles_per_k]
    wb = jnp.where(row_tile[:, :, None] == tile_id[:, None, None], dense_in[tile_id // tiles_per_k], 0.0)

    cc = jnp.stack([c_re, -c_im], axis=0)
    r_out = jnp.transpose(cc, (1, 0, 3, 2)).reshape(n_tiles, gpt, 2, n_state, p)
    r_out = jnp.transpose(r_out, (0, 2, 1, 3, 4)).reshape(n_tiles * xw, p)
    r_out = jnp.tile(r_out, (1, kt // p))
    out_row_g = (jnp.arange(n_tiles * xw) // xw) * gpt + (jnp.arange(n_tiles * xw) % LANES) // n_state
    out_col_g = jnp.arange(kt) // p
    wc = jnp.where((out_row_g % (kt // p))[:, None] == out_col_g[None, :], r_out, 0.0)
    wc = wc.reshape(n_k, tiles_per_k * xw, kt)

    a1 = (ab_re.reshape(n_tiles, LANES), ab_im.reshape(n_tiles, LANES))

    def cmul(x, y):
        return x[0] * y[0] - x[1] * y[1], x[0] * y[1] + x[1] * y[0]

    a_seg = a1
    assert seg & (seg - 1) == 0, "segment length must be a power of two"
    for _ in range(seg.bit_length() - 1):
        a_seg = cmul(a_seg, a_seg)
    pows = [a_seg]
    for _ in range(SUBLANES - 1):
        pows.append(cmul(pows[-1], a_seg))
    row = jnp.arange(SUBLANES)[None, :, None]
    planes = [jnp.broadcast_to(a1[c][:, None, :], (n_tiles, SUBLANES, LANES)) for c in range(2)]
    for k in (1, 2, 4):
        for c in range(2):
            planes.append(jnp.where(row >= k, pows[k - 1][c][:, None, :], 0.0))
    for c in range(2):
        planes.append(jnp.stack([pw[c] for pw in pows], axis=1))
    cst = jnp.stack(planes, axis=1).astype(F32)
    return wb.astype(BF16), cst, wc.astype(BF16)


def _s5(x, norm_gain, w_in, col0, a_re, a_im, log_dt, b_re, b_im, c_re, c_im, d_skip, w_glu, b_glu):
    b, l, d = x.shape
    width = d_skip.shape[0]
    tm = S5_TILE
    seg = tm // SUBLANES
    wb, cst, wc = _s5_weights(a_re, a_im, log_dt, b_re, b_im, c_re, c_im, seg)
    n_tiles = wb.shape[0]
    rp = jnp.arange(tm)
    pm = ((rp % SUBLANES) * seg + rp // SUBLANES)[:, None] == rp[None, :]
    return pl.pallas_call(
        functools.partial(_s5_kernel, col0=col0),
        grid=(l // tm,),
        in_specs=[pl.BlockSpec((b, tm, d), lambda i: (0, i, 0)), _const_spec((1, d)), _const_spec(w_in.shape),
                  _const_spec((tm, tm)), _const_spec((tm, tm)),
                  _const_spec(wb.shape), _const_spec(cst.shape), _const_spec(wc.shape),
                  _const_spec((1, width)), _const_spec((width, width)), _const_spec((1, width))],
        out_specs=pl.BlockSpec((b, tm, width), lambda i: (0, i, 0)),
        out_shape=jax.ShapeDtypeStruct((b, l, width), BF16),
        scratch_shapes=[pltpu.VMEM((b, tm, n_tiles * 2 * LANES), F32),
                        pltpu.VMEM((b, n_tiles, 2, SUBLANES, LANES), F32)],
        compiler_params=_params(1),
        name="s5",
    )(x, norm_gain.reshape(1, d), w_in.astype(BF16), pm.astype(BF16), pm.T.astype(BF16), wb, cst, wc,
      d_skip.reshape(1, width), w_glu.astype(BF16),
      b_glu.reshape(1, width))


def _hgrn2_chunk(p_ref, r0, lb, st, causal, o_ref, *, heads, chunk):
    n_rows, _, width = o_ref.shape
    e = width // heads
    rows = pl.ds(r0, chunk)
    units = [(h, r) for h in range(heads) for r in range(n_rows)]

    q_inter, k_state, q_intra, v_b, gate, decay = [], [], [], [], [], []
    for h, r in units:
        sl = slice(h * e, (h + 1) * e)
        q = p_ref[r, rows, h * e:(h + 1) * e]
        f = p_ref[r, rows, width + h * e:width + (h + 1) * e]
        lb_h = lb[:, sl]
        fg = lb_h + (1.0 - lb_h) * _sigmoid(f)
        b = _cumsum_rows(jnp.log(fg))
        b_last = b[chunk - 1:chunk, :]
        q_inter.append((q * jnp.exp(b)).astype(BF16))
        to_end = jnp.exp(b_last - b)
        k_state.append(((1.0 - fg) * to_end).astype(BF16))
        q_intra.append((q * (1.0 / to_end)).astype(BF16))
        v_b.append(p_ref[r, rows, 2 * width + h * e:2 * width + (h + 1) * e].astype(BF16))
        gate.append(p_ref[r, rows, 3 * width + h * e:3 * width + (h + 1) * e])
        decay.append(jnp.exp(b_last))

    scores = [_dot_nt(q_intra[u], k_state[u]) for u in range(len(units))]
    scores = [jnp.where(causal, sc, 0.0).astype(BF16) for sc in scores]
    outs = []
    for u, (h, r) in enumerate(units):
        s_t = st[r, h]
        outs.append(_dot(scores[u], v_b[u]) + _dot_nt(q_inter[u], s_t.astype(BF16)))
        st[r, h] = s_t * decay[u] + _dot_tn(v_b[u], k_state[u])
    for u, (h, r) in enumerate(units):
        o = outs[u]
        o = o * lax.rsqrt(jnp.mean(o * o, axis=-1, keepdims=True) + EPS)
        o_ref[r, rows, h * e:(h + 1) * e] = (o * gate[u]).astype(o_ref.dtype)


def _causal_mask(chunk):
    ti = lax.broadcasted_iota(jnp.int32, (chunk, chunk), 0)
    si = lax.broadcasted_iota(jnp.int32, (chunk, chunk), 1)
    return si <= ti


def _tail_math(xs, mix_groups, wo_ref, g1_ref, g2_ref, g3_ref, w1_ref, w3_ref, w2_ref):
    mps = []
    for mix_refs in mix_groups:
        mp = None
        row0 = 0
        for m_ref in mix_refs:
            part = _dot(m_ref[...], wo_ref[row0:row0 + m_ref.shape[1], :])
            mp = part if mp is None else mp + part
            row0 += m_ref.shape[1]
        mps.append(mp)
    x1s = [x + _rms(mp, g1_ref[...]) for x, mp in zip(xs, mps)]
    hs = [_rms(x1, g2_ref[...]).astype(BF16) for x1 in x1s]
    ys = []
    for h in hs:
        z = (_silu(_dot(h, w1_ref[...])) * _dot(h, w3_ref[...])).astype(BF16)
        ys.append(_dot(z, w2_ref[...]))
    return [x1 + _rms(y, g3_ref[...]) for x1, y in zip(x1s, ys)]


def _hgrn2_kernel(p_ref, lb_ref, o_ref, st, *, heads, chunk):
    tm = o_ref.shape[1]

    @pl.when(pl.program_id(0) == 0)
    def _init():
        st[...] = jnp.zeros_like(st)

    lb = lb_ref[...]
    causal = _causal_mask(chunk)

    def body(c, _):
        _hgrn2_chunk(p_ref, pl.multiple_of(c * chunk, chunk), lb, st, causal, o_ref, heads=heads, chunk=chunk)
        return 0

    lax.fori_loop(0, tm // chunk, body, 0, unroll=HGRN_CHUNK_UNROLL)


def _hgrn2(proj, lb):
    b, l, w4 = proj.shape
    width = w4 // 4
    heads = width // H_EXPAND
    tm = HGRN_TILE
    kern = functools.partial(_hgrn2_kernel, heads=heads, chunk=HGRN_CHUNK)
    return pl.pallas_call(
        kern,
        grid=(l // tm,),
        in_specs=[pl.BlockSpec((b, tm, w4), lambda i: (0, i, 0)), _const_spec((1, width))],
        out_specs=pl.BlockSpec((b, tm, width), lambda i: (0, i, 0)),
        out_shape=jax.ShapeDtypeStruct((b, l, width), BF16),
        scratch_shapes=[pltpu.VMEM((b, heads, H_EXPAND, H_EXPAND), F32)],
        compiler_params=_params(1),
        name="hgrn2",
    )(proj, lb.reshape(1, width))


def _tail_kernel(*refs, n_mix):
    x_ref = refs[0]
    mix_refs = refs[1:1 + n_mix]
    wo_ref, g1_ref, g2_ref, g3_ref, w1_ref, w3_ref, w2_ref, o_ref = refs[1 + n_mix:]
    rows = o_ref.shape[0] // TAIL_ROW_GROUPS
    groups = [slice(r * rows, (r + 1) * rows) for r in range(TAIL_ROW_GROUPS)]
    outs = _tail_math([x_ref[sl, :] for sl in groups], [[m.at[sl, :] for m in mix_refs] for sl in groups],
                      wo_ref, g1_ref, g2_ref, g3_ref, w1_ref, w3_ref, w2_ref)
    for sl, out in zip(groups, outs):
        o_ref[sl, :] = out


def _block_tail(x2d, mixes, w_out, g1, g2, g3, w1, w3, w2, layer):
    t, d = x2d.shape
    f = w1.shape[2]
    tm = ROW_TILE_TAIL
    n_mix = len(mixes)
    row = lambda i: (i, 0)
    layer_spec = lambda r, c: pl.BlockSpec((None, r, c), lambda i: (layer, 0, 0), pipeline_mode=pl.Buffered(1))
    in_specs = ([pl.BlockSpec((tm, d), row)]
                + [pl.BlockSpec((tm, m.shape[1]), row) for m in mixes]
                + [_const_spec(w_out.shape)]
                + [_const_spec((1, d))] * 3
                + [layer_spec(d, f), layer_spec(d, f), layer_spec(f, d)])
    return pl.pallas_call(
        functools.partial(_tail_kernel, n_mix=n_mix),
        grid=(t // tm,),
        in_specs=in_specs,
        out_specs=pl.BlockSpec((tm, d), row),
        out_shape=jax.ShapeDtypeStruct((t, d), F32),
        compiler_params=_params(1),
        name="outproj_swiglu",
    )(x2d, *mixes, w_out.astype(BF16), g1.reshape(1, d), g2.reshape(1, d), g3.reshape(1, d), w1, w3, w2)


def kernel(x, norm_g, ab_w_in, ab_conv_w, ab_conv_b, ab_wq, ab_wk, ab_wv, ab_w_if, ab_b_if, ab_mh_gain, ab_skip,
           ab_a_re, ab_a_im, ab_log_dt, ab_b_re, ab_b_im, ab_c_re, ab_c_im, ab_d, ab_w_glu, ab_b_glu, ab_w_out,
           c_w_in, c_lb_raw, c_g_gain, c_w_out, ffn_w1, ffn_w3, ffn_w2):
    bsz, seq, d = x.shape
    t = bsz * seq
    depth = norm_g.shape[0]
    x2d = x.reshape(t, d)
    lbs = jnp.cumsum(jax.nn.softmax(c_lb_raw.astype(F32), axis=0), axis=0)
    w1_b, w3_b, w2_b = ffn_w1.astype(BF16), ffn_w3.astype(BF16), ffn_w2.astype(BF16)
    for layer in range(depth):
        j = layer // 2
        if layer % 2 == 0:
            m_width = ab_conv_w.shape[2]
            s_width = ab_d.shape[1]
            x3d = x2d.reshape(bsz, seq, d)
            w_in_b = ab_w_in[j].astype(BF16)
            out_m = _mlstm(x3d, norm_g[layer, 0], w_in_b, ab_conv_w[j], ab_conv_b[j], ab_wq[j],
                           ab_wk[j], ab_wv[j], ab_w_if[j], ab_b_if[j], ab_mh_gain[j], ab_skip[j], M_HEADS)
            out_s = _s5(x3d, norm_g[layer, 0], w_in_b, 2 * m_width, ab_a_re[j], ab_a_im[j], ab_log_dt[j],
                        ab_b_re[j], ab_b_im[j], ab_c_re[j], ab_c_im[j], ab_d[j], ab_w_glu[j], ab_b_glu[j])
            mixes = [out_m.reshape(t, m_width), out_s.reshape(t, s_width)]
            w_out = ab_w_out[j]
        else:
            proj = _hgrn2_proj(x2d, norm_g[layer, 0], c_w_in[j].astype(BF16), c_g_gain[j], ROW_TILE_PROJ)
            lb = lbs[layer] - lbs[0]
            mixes = [_hgrn2(proj.reshape(bsz, seq, -1), lb).reshape(t, -1)]
            w_out = c_w_out[j]
        x2d = _block_tail(x2d, mixes, w_out, norm_g[layer, 1], norm_g[layer, 2], norm_g[layer, 3],
                          w1_b, w3_b, w2_b, layer)
    return x2d.reshape(bsz, seq, d)
```
